```python
import math
import jax, jax.numpy as jnp
from jax import lax
import numpy as np

D_MODEL = 2048
BATCH = 8
SEQ = 4096
DEPTH = 4
DEC_BATCH = 32
DEC_SEQ = 16
PAST_LEN = 4096

CHUNK = 64
N_MIXERS = 2
N_S5 = (DEPTH + 1) // 2
N_SSD = DEPTH // 2
RMS_EPS = 1e-5
S5_GROUP = 16
S5_GROUPS = D_MODEL // S5_GROUP
S5_STATE = 64
SSD_INNER = 2 * D_MODEL
SSD_HEAD_DIM = 64
SSD_HEADS = SSD_INNER // SSD_HEAD_DIM
SSD_GROUPS = 8
SSD_HPG = SSD_HEADS // SSD_GROUPS
SSD_STATE = 128
SSD_CONV = 4
SSD_CONV_DIM = SSD_INNER + 2 * SSD_GROUPS * SSD_STATE
SSD_PROJ = SSD_INNER + SSD_CONV_DIM + SSD_HEADS
D_FF = 4 * D_MODEL

kernel_name = 'hybrid_s5_ssd_streaming_step'

F32 = jnp.float32


def rmsnorm(x, g):
    xf = x.astype(F32)
    y = xf * lax.rsqrt(jnp.mean(xf * xf, axis=-1, keepdims=True) + RMS_EPS)
    return (y * g.astype(F32)).astype(x.dtype)


def blockwise(step, carry, xs):
    t = xs[0].shape[1]
    if t <= CHUNK:
        return step(carry, xs)
    nb = t // CHUNK
    split = lambda a: jnp.moveaxis(a.reshape(a.shape[0], nb, CHUNK, *a.shape[2:]), 1, 0)
    carry, ys = lax.scan(step, carry, tuple(split(a) for a in xs))
    merge = lambda a: jnp.moveaxis(a, 0, 1).reshape(a.shape[1], t, *a.shape[3:])
    return carry, merge(ys)


def s5_mixer(u, h0, lam_re, lam_im, log_dt, b_re, b_im, c_re, c_im, d_skip, w_glu):
    bsz, t, _ = u.shape
    lam = lax.complex(lam_re.astype(F32), lam_im.astype(F32))
    dt = jnp.exp(log_dt.astype(F32))[:, None]
    lam_bar = jnp.exp(lam * dt)
    b = lax.complex(b_re.astype(F32), b_im.astype(F32))
    b_bar = ((lam_bar - 1.0) / lam)[..., None] * b
    c = lax.complex(c_re.astype(F32), c_im.astype(F32))
    uf = u.astype(F32)
    ug = uf.reshape(bsz, t, S5_GROUPS, S5_GROUP)

    def combine(left, right):
        a_l, b_l = left
        a_r, b_r = right
        return a_l * a_r, a_r * b_l + b_r

    def step(h, blk):
        ub = blk[0]
        bu = jnp.einsum('blgc,gpc->blgp', ub.astype(jnp.complex64), b_bar)
        bu = bu.at[:, 0].add(lam_bar * h)
        a = jnp.broadcast_to(lam_bar, bu.shape)
        _, hs = lax.associative_scan(combine, (a, bu), axis=1)
        y = jnp.real(jnp.einsum('blgp,gcp->blgc', hs, c))
        return hs[:, -1], y

    h_last, y = blockwise(step, h0, (ug,))
    y = y.reshape(bsz, t, D_MODEL) + d_skip.astype(F32) * uf
    g = jax.nn.gelu(y).astype(w_glu.dtype)
    ga, gb = jnp.split(g @ w_glu, 2, axis=-1)
    out = ga.astype(F32) * jax.nn.sigmoid(gb.astype(F32))
    return out, h_last


def ssd_mixer(u, conv_state, ssm_state, w_in, conv_w, conv_b, dt_bias, a_log, d_skip, norm_w, w_out):
    bsz, t, _ = u.shape
    zxbcdt = u @ w_in
    z, xbc, dt_raw = jnp.split(zxbcdt, [SSD_INNER, SSD_INNER + SSD_CONV_DIM], axis=-1)
    xpad = jnp.concatenate([conv_state.astype(F32), xbc.astype(F32)], axis=1)
    new_conv = xpad[:, -(SSD_CONV - 1):]
    cw = conv_w.astype(F32)
    xc = conv_b.astype(F32) + sum(xpad[:, k:k + t] * cw[k] for k in range(SSD_CONV))
    xc = jax.nn.silu(xc)
    xs, bm, cm = jnp.split(xc, [SSD_INNER, SSD_INNER + SSD_GROUPS * SSD_STATE], axis=-1)
    xs = xs.reshape(bsz, t, SSD_GROUPS, SSD_HPG, SSD_HEAD_DIM)
    bm = bm.reshape(bsz, t, SSD_GROUPS, SSD_STATE)
    cm = cm.reshape(bsz, t, SSD_GROUPS, SSD_STATE)
    dt = jax.nn.softplus(dt_raw.astype(F32) + dt_bias.astype(F32)).reshape(bsz, t, SSD_GROUPS, SSD_HPG)
    a = -jnp.exp(a_log.astype(F32)).reshape(SSD_GROUPS, SSD_HPG)
    h0 = ssm_state.astype(F32).reshape(bsz, SSD_GROUPS, SSD_HPG, SSD_HEAD_DIM, SSD_STATE)

    def step(h, blk):
        x_b, b_b, c_b, dt_b = blk
        l = x_b.shape[1]
        acum = jnp.cumsum(dt_b * a, axis=1)
        diff = acum[:, :, None] - acum[:, None, :]
        causal = jnp.tril(jnp.ones((l, l), dtype=bool))[None, :, :, None, None]
        decay = jnp.exp(jnp.where(causal, diff, -jnp.inf))
        cb = jnp.einsum('blgn,bsgn->blsg', c_b, b_b)
        w = cb[..., None] * decay * dt_b[:, None]
        y = jnp.einsum('blsgk,bsgkp->blgkp', w, x_b)
        y = y + jnp.einsum('blgn,bgkpn->blgkp', c_b, h) * jnp.exp(acum)[..., None]
        tail = jnp.exp(acum[:, -1:] - acum) * dt_b
        h_new = h * jnp.exp(acum[:, -1])[..., None, None] + jnp.einsum('bsgn,bsgk,bsgkp->bgkpn', b_b, tail, x_b)
        return h_new, y

    h_last, y = blockwise(step, h0, (xs, bm, cm, dt))
    y = y + d_skip.astype(F32).reshape(SSD_GROUPS, SSD_HPG)[..., None] * xs
    y = y.reshape(bsz, t, SSD_INNER) * jax.nn.silu(z.astype(F32))
    yg = y.reshape(bsz, t, SSD_GROUPS, SSD_INNER // SSD_GROUPS)
    yg = yg * lax.rsqrt(jnp.mean(yg * yg, axis=-1, keepdims=True) + RMS_EPS)
    y = yg.reshape(bsz, t, SSD_INNER) * norm_w.astype(F32)
    out = y.astype(w_out.dtype) @ w_out
    return out, new_conv, h_last.reshape(bsz, SSD_HEADS, SSD_HEAD_DIM, SSD_STATE)


def channel_mlp(h, w1, w2):
    a = jax.nn.relu(h @ w1)
    return (a * a) @ w2


def setup_inputs(seed: int = 0) -> dict:
    key = jax.random.key(seed)
    ks = iter(jax.random.split(key, 40))
    nrm = lambda shape, scale: scale * jax.random.normal(next(ks), shape, F32)
    n_idx = jnp.arange(S5_STATE, dtype=F32)
    dt_ssd = jnp.exp(jax.random.uniform(next(ks), (N_SSD, SSD_HEADS), F32, math.log(1e-3), math.log(1e-1)))
    return {
        'x_prompt': nrm((BATCH, SEQ, D_MODEL), 1.0),
        'x_sample': nrm((DEC_BATCH, DEC_SEQ, D_MODEL), 1.0),
        'state_s5_re': nrm((N_S5, DEC_BATCH, S5_GROUPS, S5_STATE), 0.1),
        'state_s5_im': nrm((N_S5, DEC_BATCH, S5_GROUPS, S5_STATE), 0.1),
        'state_ssd_conv': nrm((N_SSD, DEC_BATCH, SSD_CONV - 1, SSD_CONV_DIM), 1.0),
        'state_ssd_ssm': nrm((N_SSD, DEC_BATCH, SSD_HEADS, SSD_HEAD_DIM, SSD_STATE), 0.1),
        'norm_mix': 1.0 + nrm((DEPTH, D_MODEL), 0.02),
        'norm_mlp': 1.0 + nrm((DEPTH, D_MODEL), 0.02),
        'norm_final': 1.0 + nrm((D_MODEL,), 0.02),
        's5_lam_re': -0.5 + nrm((N_S5, S5_GROUPS, S5_STATE), 0.01),
        's5_lam_im': math.pi * n_idx + nrm((N_S5, S5_GROUPS, S5_STATE), 0.01),
        's5_log_dt': jax.random.uniform(next(ks), (N_S5, S5_GROUPS), F32, math.log(1e-3), math.log(1e-1)),
        's5_b_re': nrm((N_S5, S5_GROUPS, S5_STATE, S5_GROUP), (2 * S5_GROUP) ** -0.5),
        's5_b_im': nrm((N_S5, S5_GROUPS, S5_STATE, S5_GROUP), (2 * S5_GROUP) ** -0.5),
        's5_c_re': nrm((N_S5, S5_GROUPS, S5_GROUP, S5_STATE), S5_STATE ** -0.5),
        's5_c_im': nrm((N_S5, S5_GROUPS, S5_GROUP, S5_STATE), S5_STATE ** -0.5),
        's5_d': 1.0 + nrm((N_S5, D_MODEL), 0.1),
        's5_w_glu': nrm((N_S5, D_MODEL, 2 * D_MODEL), D_MODEL ** -0.5),
        'ssd_w_in': nrm((N_SSD, D_MODEL, SSD_PROJ), D_MODEL ** -0.5),
        'ssd_conv_w': nrm((N_SSD, SSD_CONV, SSD_CONV_DIM), SSD_CONV ** -0.5),
        'ssd_conv_b': nrm((N_SSD, SSD_CONV_DIM), 0.02),
        'ssd_dt_bias': dt_ssd + jnp.log(-jnp.expm1(-dt_ssd)),
        'ssd_a_log': jnp.log(jax.random.uniform(next(ks), (N_SSD, SSD_HEADS), F32, 1.0, 16.0)),
        'ssd_d': 1.0 + nrm((N_SSD, SSD_HEADS), 0.1),
        'ssd_norm': 1.0 + nrm((N_SSD, SSD_INNER), 0.02),
        'ssd_w_out': nrm((N_SSD, SSD_INNER, D_MODEL), SSD_INNER ** -0.5),
        'mlp_w1': nrm((DEPTH, D_MODEL, D_FF), D_MODEL ** -0.5),
        'mlp_w2': nrm((DEPTH, D_FF, D_MODEL), D_FF ** -0.5),
    }


def reference(x_prompt, x_sample, state_s5_re, state_s5_im, state_ssd_conv, state_ssd_ssm,
              norm_mix, norm_mlp, norm_final,
              s5_lam_re, s5_lam_im, s5_log_dt, s5_b_re, s5_b_im, s5_c_re, s5_c_im, s5_d, s5_w_glu,
              ssd_w_in, ssd_conv_w, ssd_conv_b, ssd_dt_bias, ssd_a_log, ssd_d, ssd_norm, ssd_w_out,
              mlp_w1, mlp_w2):
    def run(x, s5_h, conv_st, ssm_st):
        s5_out, conv_out, ssm_out = [], [], []
        for i in range(DEPTH):
            j = i // N_MIXERS
            u = rmsnorm(x, norm_mix[i])
            if i % N_MIXERS == 0:
                m, h = s5_mixer(u, s5_h[j], s5_lam_re[j], s5_lam_im[j], s5_log_dt[j], s5_b_re[j], s5_b_im[j],
                                s5_c_re[j], s5_c_im[j], s5_d[j], s5_w_glu[j])
                s5_out.append(h)
            else:
                m, c, s = ssd_mixer(u, conv_st[j], ssm_st[j], ssd_w_in[j], ssd_conv_w[j], ssd_conv_b[j],
                                    ssd_dt_bias[j], ssd_a_log[j], ssd_d[j], ssd_norm[j], ssd_w_out[j])
                conv_out.append(c)
                ssm_out.append(s)
            x = x + m.astype(x.dtype)
            x = x + channel_mlp(rmsnorm(x, norm_mlp[i]), mlp_w1[i], mlp_w2[i]).astype(x.dtype)
        s5_all = jnp.stack(s5_out)
        return rmsnorm(x, norm_final), jnp.real(s5_all), jnp.imag(s5_all), jnp.stack(conv_out), jnp.stack(ssm_out)

    bp = x_prompt.shape[0]
    s5_zero = jnp.zeros((N_S5, bp, S5_GROUPS, S5_STATE), jnp.complex64)
    conv_zero = jnp.zeros((N_SSD, bp, SSD_CONV - 1, SSD_CONV_DIM), F32)
    ssm_zero = jnp.zeros((N_SSD, bp, SSD_HEADS, SSD_HEAD_DIM, SSD_STATE), F32)
    y_prompt, s5_re_p, s5_im_p, conv_p, ssm_p = run(x_prompt, s5_zero, conv_zero, ssm_zero)

    s5_cache = lax.complex(state_s5_re.astype(F32), state_s5_im.astype(F32))
    y_sample, s5_re_s, s5_im_s, conv_s, ssm_s = run(x_sample, s5_cache, state_ssd_conv, state_ssd_ssm)
    return (y_prompt, y_sample, s5_re_p, s5_im_p, conv_p, ssm_p, s5_re_s, s5_im_s, conv_s, ssm_s)
```

```python
import functools

import jax
import jax.numpy as jnp
from jax import lax
from jax.experimental import pallas as pl
from jax.experimental.pallas import tpu as pltpu

F32 = jnp.float32
BF16 = jnp.bfloat16
HIGHEST = lax.Precision.HIGHEST

D_MODEL = 2048
DEPTH = 4
RMS_EPS = 1e-5
S5_GROUP = 16
S5_GROUPS = D_MODEL // S5_GROUP
S5_STATE = 64
S5_SLABS = 8
S5_SLAB_CH = D_MODEL // S5_SLABS
S5_SLAB_ST = (S5_GROUPS // S5_SLABS) * S5_STATE
SSD_INNER = 2 * D_MODEL
SSD_HEAD_DIM = 64
SSD_HEADS = SSD_INNER // SSD_HEAD_DIM
SSD_GROUPS = 8
SSD_STATE = 128
SSD_CONV = 4
SSD_CONV_DIM = SSD_INNER + 2 * SSD_GROUPS * SSD_STATE
SSD_GROUP_CH = SSD_INNER // SSD_GROUPS
LANES = 128
SUBLANES = 8
SSD_PROJ_PAD = 10752
D_FF = 4 * D_MODEL

VMEM_LIMIT = 56 * 1024 * 1024

NT_DIMS = (((1,), (1,)), ((), ()))


def _cparams(n_axes):
    return pltpu.CompilerParams(
        dimension_semantics=("arbitrary",) * n_axes, vmem_limit_bytes=VMEM_LIMIT)


def _rms(x, g):
    ms = jnp.mean(x * x, axis=-1, keepdims=True)
    return x * lax.rsqrt(ms + RMS_EPS) * g


def _row_tile(m, want):
    t = min(m, want)
    assert m % t == 0, (m, t)
    return t


def _rmsnorm_kernel(x_ref, g_ref, o_ref):
    o_ref[...] = _rms(x_ref[...], g_ref[...])


def _rmsnorm(x, g, tm=512):
    m, d = x.shape
    tm = _row_tile(m, tm)
    return pl.pallas_call(
        _rmsnorm_kernel,
        grid=(m // tm,),
        in_specs=[pl.BlockSpec((tm, d), lambda i: (i, 0)),
                  pl.BlockSpec((1, d), lambda i: (0, 0))],
        out_specs=pl.BlockSpec((tm, d), lambda i: (i, 0)),
        out_shape=jax.ShapeDtypeStruct((m, d), F32),
        compiler_params=_cparams(1),
        name="rmsnorm",
    )(x, g.reshape(1, d))


def _norm_matmul_kernel(x_ref, g_ref, w_ref, o_ref, u_ref):
    @pl.when(pl.program_id(1) == 0)
    def _():
        u_ref[...] = _rms(x_ref[...], g_ref[...]).astype(BF16)

    o_ref[...] = jnp.dot(u_ref[...], w_ref[...], preferred_element_type=F32)


def _norm_matmul(x, g, w, tm=512, tn=512):
    m, d = x.shape
    n = w.shape[1]
    tm = _row_tile(m, tm)
    return pl.pallas_call(
        _norm_matmul_kernel,
        grid=(m // tm, n // tn),
        in_specs=[pl.BlockSpec((tm, d), lambda i, j: (i, 0)),
                  pl.BlockSpec((1, d), lambda i, j: (0, 0)),
                  pl.BlockSpec((d, tn), lambda i, j: (0, j))],
        out_specs=pl.BlockSpec((tm, tn), lambda i, j: (i, j)),
        out_shape=jax.ShapeDtypeStruct((m, n), F32),
        scratch_shapes=[pltpu.VMEM((tm, d), BF16)],
        compiler_params=_cparams(2),
        name="norm_matmul",
    )(x, g.reshape(1, d), w)


def _mlp_kernel(x_ref, g_ref, w1_ref, w2_ref, gf_ref, o_ref, u_ref, *, nj, final_norm):
    j = pl.program_id(1)

    @pl.when(j == 0)
    def _():
        x = x_ref[...]
        u_ref[...] = _rms(x, g_ref[...]).astype(BF16)
        o_ref[...] = x

    a = jnp.dot(u_ref[...], w1_ref[...], preferred_element_type=F32)
    a = jnp.maximum(a, 0.0)
    a = (a * a).astype(BF16)
    o_ref[...] += jnp.dot(a, w2_ref[...], preferred_element_type=F32)

    if final_norm:
        @pl.when(j == nj - 1)
        def _():
            o_ref[...] = _rms(o_ref[...], gf_ref[...])


def _mlp(x, g, w1, w2, gf, final_norm, tm=512, tf=1024):
    m, d = x.shape
    f = w1.shape[1]
    tm = _row_tile(m, tm)
    nj = f // tf
    return pl.pallas_call(
        functools.partial(_mlp_kernel, nj=nj, final_norm=final_norm),
        grid=(m // tm, nj),
        in_specs=[pl.BlockSpec((tm, d), lambda i, j: (i, 0)),
                  pl.BlockSpec((1, d), lambda i, j: (0, 0)),
                  pl.BlockSpec((d, tf), lambda i, j: (0, j)),
                  pl.BlockSpec((tf, d), lambda i, j: (j, 0)),
                  pl.BlockSpec((1, d), lambda i, j: (0, 0))],
        out_specs=pl.BlockSpec((tm, d), lambda i, j: (i, 0)),
        out_shape=jax.ShapeDtypeStruct((m, d), F32),
        scratch_shapes=[pltpu.VMEM((tm, d), BF16)],
        compiler_params=_cparams(2),
        name="mlp",
    )(x, g.reshape(1, d), w1, w2, gf.reshape(1, d))


def _matmul_residual_kernel(x_ref, a_ref, w_ref, o_ref):
    o_ref[...] = x_ref[...] + jnp.dot(a_ref[...], w_ref[...], preferred_element_type=F32)


def _matmul_residual(x, a, w, tm=512, tn=512):
    m, n = x.shape
    kdim = a.shape[1]
    tm = _row_tile(m, tm)
    return pl.pallas_call(
        _matmul_residual_kernel,
        grid=(m // tm, n // tn),
        in_specs=[pl.BlockSpec((tm, tn), lambda i, j: (i, j)),
                  pl.BlockSpec((tm, kdim), lambda i, j: (i, 0)),
                  pl.BlockSpec((kdim, tn), lambda i, j: (0, j))],
        out_specs=pl.BlockSpec((tm, tn), lambda i, j: (i, j)),
        out_shape=jax.ShapeDtypeStruct((m, n), F32),
        compiler_params=_cparams(2),
        name="matmul_residual",
    )(x, a, w)


def _glu_residual_kernel(x_ref, a_ref, wa_ref, wb_ref, o_ref):
    a = a_ref[...]
    ga = jnp.dot(a, wa_ref[...], preferred_element_type=F32)
    gb = jnp.dot(a, wb_ref[...], preferred_element_type=F32)
    o_ref[...] = x_ref[...] + ga * jax.nn.sigmoid(gb)


def _glu_residual(x, a, w, tm=512, tn=512):
    m, n = x.shape
    kdim = a.shape[1]
    tm = _row_tile(m, tm)
    nb = n // tn
    return pl.pallas_call(
        _glu_residual_kernel,
        grid=(m // tm, nb),
        in_specs=[pl.BlockSpec((tm, tn), lambda i, j: (i, j)),
                  pl.BlockSpec((tm, kdim), lambda i, j: (i, 0)),
                  pl.BlockSpec((kdim, tn), lambda i, j: (0, j)),
                  pl.BlockSpec((kdim, tn), lambda i, j: (0, j + nb))],
        out_specs=pl.BlockSpec((tm, tn), lambda i, j: (i, j)),
        out_shape=jax.ShapeDtypeStruct((m, n), F32),
        compiler_params=_cparams(2),
        name="glu_residual",
    )(x, a, w, w)


S5_SCAN_CHUNKS = 2


def _s5_kernel(u_ref, wb_ref, wc_ref, lam_ref, d_ref, h0_ref, g_ref, hf_ref, hbuf_ref, hs_ref,
               *, bsz, tt, nk):
    k = pl.program_id(1)

    @pl.when(k == 0)
    def _():
        hs_ref[...] = h0_ref[0]

    u = u_ref[...]
    hbuf_ref[...] = jnp.dot(u.astype(BF16), wb_ref[0], preferred_element_type=F32)

    cw = S5_SLAB_ST // S5_SCAN_CHUNKS
    for q in range(S5_SCAN_CHUNKS):
        re = slice(q * cw, (q + 1) * cw)
        im = slice(S5_SLAB_ST + q * cw, S5_SLAB_ST + (q + 1) * cw)
        lr = lam_ref[0, 0:1, re]
        li = lam_ref[0, 1:2, re]

        def step(t, carry, re=re, im=im, lr=lr, li=li):
            hr, hi = carry
            rows = pl.ds(pl.multiple_of(t * bsz, bsz), bsz)
            nr = lr * hr - li * hi + hbuf_ref[rows, re]
            ni = lr * hi + li * hr + hbuf_ref[rows, im]
            hbuf_ref[rows, re] = nr
            hbuf_ref[rows, im] = ni
            return nr, ni

        hr, hi = lax.fori_loop(0, tt, step, (hs_ref[:, re], hs_ref[:, im]), unroll=4)
        hs_ref[:, re] = hr
        hs_ref[:, im] = hi

    y = jnp.dot(hbuf_ref[...].astype(BF16), wc_ref[0], preferred_element_type=F32)
    y = y + d_ref[...] * u
    g_ref[...] = jax.nn.gelu(y).astype(BF16)

    @pl.when(k == nk - 1)
    def _():
        hf_ref[0] = hs_ref[...]


def _s5_core(ut, wb, wc, lam, d, h0, bsz, tlen):
    tt = min(tlen, 512 // bsz)
    rows = tt * bsz
    nk = tlen // tt
    return pl.pallas_call(
        functools.partial(_s5_kernel, bsz=bsz, tt=tt, nk=nk),
        grid=(S5_SLABS, nk),
        in_specs=[pl.BlockSpec((rows, S5_SLAB_CH), lambda s, k: (k, s)),
                  pl.BlockSpec((1, S5_SLAB_CH, 2 * S5_SLAB_ST), lambda s, k: (s, 0, 0)),
                  pl.BlockSpec((1, 2 * S5_SLAB_ST, S5_SLAB_CH), lambda s, k: (s, 0, 0)),
                  pl.BlockSpec((1, 2, S5_SLAB_ST), lambda s, k: (s, 0, 0)),
                  pl.BlockSpec((1, S5_SLAB_CH), lambda s, k: (0, s)),
                  pl.BlockSpec((1, bsz, 2 * S5_SLAB_ST), lambda s, k: (s, 0, 0))],
        out_specs=[pl.BlockSpec((rows, S5_SLAB_CH), lambda s, k: (k, s)),
                   pl.BlockSpec((1, bsz, 2 * S5_SLAB_ST), lambda s, k: (s, 0, 0))],
        out_shape=[jax.ShapeDtypeStruct((tlen * bsz, D_MODEL), BF16),
                   jax.ShapeDtypeStruct((S5_SLABS, bsz, 2 * S5_SLAB_ST), F32)],
        scratch_shapes=[pltpu.VMEM((rows, 2 * S5_SLAB_ST), F32),
                        pltpu.VMEM((bsz, 2 * S5_SLAB_ST), F32)],
        compiler_params=_cparams(2),
        name="s5_core",
    )(ut, wb, wc, lam, d.reshape(1, D_MODEL), h0)


def _s5_params(lam_re, lam_im, log_dt, b_re, b_im, c_re, c_im):
    dt = jnp.exp(log_dt)[:, None]
    er = jnp.exp(lam_re * dt)
    lbr = er * jnp.cos(lam_im * dt)
    lbi = er * jnp.sin(lam_im * dt)
    den = lam_re * lam_re + lam_im * lam_im
    qr = ((lbr - 1.0) * lam_re + lbi * lam_im) / den
    qi = (lbi * lam_re - (lbr - 1.0) * lam_im) / den
    bbr = qr[..., None] * b_re - qi[..., None] * b_im
    bbi = qr[..., None] * b_im + qi[..., None] * b_re
    gl = S5_GROUPS // S5_SLABS
    eye = jnp.eye(gl, dtype=F32)

    def pack_b(bb):
        bb = bb.reshape(S5_SLABS, gl, S5_STATE, S5_GROUP)
        w = jnp.einsum("sgpc,gh->sgchp", bb, eye)
        return w.reshape(S5_SLABS, S5_SLAB_CH, S5_SLAB_ST)

    def pack_c(cc):
        cc = cc.reshape(S5_SLABS, gl, S5_GROUP, S5_STATE)
        w = jnp.einsum("sgcp,gh->sgphc", cc, eye)
        return w.reshape(S5_SLABS, S5_SLAB_ST, S5_SLAB_CH)

    wb = jnp.concatenate([pack_b(bbr), pack_b(bbi)], axis=2).astype(BF16)
    wc = jnp.concatenate([pack_c(c_re), pack_c(-c_im)], axis=1).astype(BF16)
    lam = jnp.stack([lbr.reshape(S5_SLABS, S5_SLAB_ST), lbi.reshape(S5_SLABS, S5_SLAB_ST)], axis=1)
    return wb, wc, lam


def _s5_state_in(h):
    bsz = h.shape[0]
    return h.reshape(bsz, S5_SLABS, S5_SLAB_ST).transpose(1, 0, 2)


def _s5_state_out(h):
    bsz = h.shape[1]
    return h.transpose(1, 0, 2).reshape(bsz, S5_GROUPS, S5_STATE)


def _s5_layer(x, bsz, tlen, h_re, h_im, norm_g, prm, d_skip, w_glu):
    wb, wc, lam = prm
    u = _rmsnorm(x, norm_g)
    ut = u.reshape(bsz, tlen, D_MODEL).transpose(1, 0, 2).reshape(tlen * bsz, D_MODEL)
    h0 = jnp.concatenate([_s5_state_in(h_re), _s5_state_in(h_im)], axis=2)
    gt, hf = _s5_core(ut, wb, wc, lam, d_skip, h0, bsz, tlen)
    g = gt.reshape(tlen, bsz, D_MODEL).transpose(1, 0, 2).reshape(bsz * tlen, D_MODEL)
    x = _glu_residual(x, g, w_glu)
    return x, _s5_state_out(hf[:, :, :S5_SLAB_ST]), _s5_state_out(hf[:, :, S5_SLAB_ST:])


def _softplus(x):
    return jnp.maximum(x, 0.0) + jnp.log1p(jnp.exp(-jnp.abs(x)))


def _lane_bcast(v, col, width):
    return jnp.broadcast_to(v[:, col:col + 1], (v.shape[0], width))


def _ssd_kernel(xbc_ref, z0_ref, z1_ref, dt_ref, cst_ref, sst_ref, cw_ref, cb_ref, dtb_ref,
                alog_ref, dsk_ref, nw_ref, y_ref, cso_ref, sso_ref,
                xp_ref, st_ref, xs_ref, yb_ref, *, clen, nk):
    k = pl.program_id(1)
    hdr = SUBLANES

    @pl.when(k == 0)
    def _():
        xp_ref[0:hdr, :] = cst_ref[0]
        st_ref[...] = sst_ref[0]

    xp_ref[hdr:hdr + clen, :] = xbc_ref[...]
    xc = cb_ref[...]
    for kk in range(SSD_CONV):
        r0 = hdr - (SSD_CONV - 1) + kk
        xc = xc + cw_ref[kk:kk + 1, :] * xp_ref[r0:r0 + clen, :]
    xc = xc * jax.nn.sigmoid(xc)
    xp_ref[0:hdr, :] = xp_ref[clen:clen + hdr, :]
    xs_ref[...] = xc[:, :SSD_INNER]
    bmat = xc[:, SSD_INNER:SSD_INNER + SSD_GROUPS * SSD_STATE].astype(BF16)
    cmat = xc[:, SSD_INNER + SSD_GROUPS * SSD_STATE:].astype(BF16)

    dt = _softplus(dt_ref[...] + dtb_ref[...])
    da = dt * (-jnp.exp(alog_ref[...]))
    ri = lax.broadcasted_iota(jnp.int32, (clen, clen), 0)
    ci = lax.broadcasted_iota(jnp.int32, (clen, clen), 1)
    causal = ri >= ci
    acum = jnp.dot(causal.astype(F32), da, precision=HIGHEST, preferred_element_type=F32)
    eye = (lax.broadcasted_iota(jnp.int32, (LANES, LANES), 0)
           == lax.broadcasted_iota(jnp.int32, (LANES, LANES), 1)).astype(F32)
    acum_t = lax.dot_general(eye, acum, NT_DIMS, precision=HIGHEST, preferred_element_type=F32)
    dt_t = lax.dot_general(eye, dt, NT_DIMS, precision=HIGHEST, preferred_element_type=F32)
    eacum = jnp.exp(acum)
    tailw = jnp.exp(acum[clen - 1:clen, :] - acum) * dt
    sdec = jnp.exp(acum_t[:, clen - 1:clen])

    lo_lane = lax.broadcasted_iota(jnp.int32, (clen, LANES), 1) < SSD_HEAD_DIM
    lo_row = lax.broadcasted_iota(jnp.int32, (LANES, SSD_STATE), 0) < SSD_HEAD_DIM
    neg_inf = jnp.float32(-jnp.inf)

    for g in range(SSD_GROUPS):
        gs = slice(g * SSD_STATE, (g + 1) * SSD_STATE)
        cg = cmat[:, gs]
        bg = bmat[:, gs]
        cbm = lax.dot_general(cg, bg, NT_DIMS, preferred_element_type=F32)
        sg = st_ref[g * SSD_GROUP_CH:(g + 1) * SSD_GROUP_CH, :].astype(BF16)
        chs = lax.dot_general(cg, sg, NT_DIMS, preferred_element_type=F32)
        for j in range(SSD_GROUP_CH // LANES):
            h0 = (g * SSD_GROUP_CH + j * LANES) // SSD_HEAD_DIM
            h1 = h0 + 1
            cs = slice(g * SSD_GROUP_CH + j * LANES, g * SSD_GROUP_CH + (j + 1) * LANES)
            xf = xs_ref[:, cs]
            xb = xf.astype(BF16)
            ys = []
            for h in (h0, h1):
                diff = _lane_bcast(acum, h, clen) - acum_t[h:h + 1, :]
                w = cbm * jnp.exp(jnp.where(causal, diff, neg_inf)) * dt_t[h:h + 1, :]
                ys.append(jnp.dot(w.astype(BF16), xb, preferred_element_type=F32))
            y = jnp.where(lo_lane, ys[0], ys[1])
            ea = jnp.where(lo_lane, _lane_bcast(eacum, h0, LANES), _lane_bcast(eacum, h1, LANES))
            y = y + chs[:, j * LANES:(j + 1) * LANES] * ea + dsk_ref[:, cs] * xf
            yb_ref[:, cs] = y
            tw = jnp.where(lo_lane, _lane_bcast(tailw, h0, LANES), _lane_bcast(tailw, h1, LANES))
            txb = (xf * tw).astype(BF16)
            ds = lax.dot_general(txb, bg, (((0,), (0,)), ((), ())), preferred_element_type=F32)
            dec = jnp.where(lo_row,
                            jnp.broadcast_to(sdec[h0:h0 + 1, :], (LANES, SSD_STATE)),
                            jnp.broadcast_to(sdec[h1:h1 + 1, :], (LANES, SSD_STATE)))
            st_ref[cs, :] = st_ref[cs, :] * dec + ds

    half = SSD_INNER // 2
    for hf, z_ref in enumerate((z0_ref, z1_ref)):
        z = z_ref[...]
        yh = yb_ref[:, hf * half:(hf + 1) * half] * (z * jax.nn.sigmoid(z))
        for gg in range(half // SSD_GROUP_CH):
            yg = yh[:, gg * SSD_GROUP_CH:(gg + 1) * SSD_GROUP_CH]
            ms = jnp.mean(yg * yg, axis=-1, keepdims=True)
            cs = slice(hf * half + gg * SSD_GROUP_CH, hf * half + (gg + 1) * SSD_GROUP_CH)
            y_ref[:, cs] = (yg * lax.rsqrt(ms + RMS_EPS) * nw_ref[:, cs]).astype(BF16)

    @pl.when(k == nk - 1)
    def _():
        cso_ref[0] = xp_ref[0:hdr, :]
        sso_ref[0] = st_ref[...]


def _ssd_core(proj, cst, sst, conv_w, conv_b, dt_bias, a_log, d_exp, norm_w, bsz, tlen):
    clen = min(tlen, 128)
    nk = tlen // clen
    zblk = SSD_INNER // 2
    z_off = SSD_CONV_DIM // zblk
    dt_off = (SSD_CONV_DIM + SSD_INNER) // LANES
    row = lambda b, k: b * nk + k
    pad = lambda v: jnp.pad(v, (0, LANES - SSD_HEADS)).reshape(1, LANES)
    return pl.pallas_call(
        functools.partial(_ssd_kernel, clen=clen, nk=nk),
        grid=(bsz, nk),
        in_specs=[pl.BlockSpec((clen, SSD_CONV_DIM), lambda b, k: (row(b, k), 0)),
                  pl.BlockSpec((clen, zblk), lambda b, k: (row(b, k), z_off)),
                  pl.BlockSpec((clen, zblk), lambda b, k: (row(b, k), z_off + 1)),
                  pl.BlockSpec((clen, LANES), lambda b, k: (row(b, k), dt_off)),
                  pl.BlockSpec((1, SUBLANES, SSD_CONV_DIM), lambda b, k: (b, 0, 0)),
                  pl.BlockSpec((1, SSD_INNER, SSD_STATE), lambda b, k: (b, 0, 0)),
                  pl.BlockSpec((SSD_CONV, SSD_CONV_DIM), lambda b, k: (0, 0)),
                  pl.BlockSpec((1, SSD_CONV_DIM), lambda b, k: (0, 0)),
                  pl.BlockSpec((1, LANES), lambda b, k: (0, 0)),
                  pl.BlockSpec((1, LANES), lambda b, k: (0, 0)),
                  pl.BlockSpec((1, SSD_INNER), lambda b, k: (0, 0)),
                  pl.BlockSpec((1, SSD_INNER), lambda b, k: (0, 0))],
        out_specs=[pl.BlockSpec((clen, SSD_INNER), lambda b, k: (row(b, k), 0)),
                   pl.BlockSpec((1, SUBLANES, SSD_CONV_DIM), lambda b, k: (b, 0, 0)),
                   pl.BlockSpec((1, SSD_INNER, SSD_STATE), lambda b, k: (b, 0, 0))],
        out_shape=[jax.ShapeDtypeStruct((bsz * tlen, SSD_INNER), BF16),
                   jax.ShapeDtypeStruct((bsz, SUBLANES, SSD_CONV_DIM), F32),
                   jax.ShapeDtypeStruct((bsz, SSD_INNER, SSD_STATE), F32)],
        scratch_shapes=[pltpu.VMEM((clen + SUBLANES, SSD_CONV_DIM), F32),
                        pltpu.VMEM((SSD_INNER, SSD_STATE), F32),
                        pltpu.VMEM((clen, SSD_INNER), F32),
                        pltpu.VMEM((clen, SSD_INNER), F32)],
        compiler_params=_cparams(2),
        name="ssd_core",
    )(proj, proj, proj, proj, cst, sst, conv_w, conv_b.reshape(1, SSD_CONV_DIM),
      pad(dt_bias), pad(a_log), d_exp.reshape(1, SSD_INNER), norm_w.reshape(1, SSD_INNER))


def _ssd_in_weight(w_in):
    d = w_in.shape[0]
    z = w_in[:, :SSD_INNER]
    xbc = w_in[:, SSD_INNER:SSD_INNER + SSD_CONV_DIM]
    dt = w_in[:, SSD_INNER + SSD_CONV_DIM:]
    padw = SSD_PROJ_PAD - (SSD_CONV_DIM + SSD_INNER + SSD_HEADS)
    return jnp.concatenate([xbc, z, dt, jnp.zeros((d, padw), w_in.dtype)], axis=1).astype(BF16)


def _ssd_layer(x, bsz, tlen, conv_st, ssm_st, norm_g, w_in_p, conv_w, conv_b, dt_bias, a_log,
               d_skip, norm_w, w_out):
    proj = _norm_matmul(x, norm_g, w_in_p)
    hist = SSD_CONV - 1
    cst = jnp.pad(conv_st, ((0, 0), (SUBLANES - hist, 0), (0, 0)))
    sst = ssm_st.reshape(bsz, SSD_INNER, SSD_STATE)
    d_exp = jnp.repeat(d_skip, SSD_HEAD_DIM)
    y, cso, sso = _ssd_core(proj, cst, sst, conv_w, conv_b, dt_bias, a_log, d_exp, norm_w, bsz, tlen)
    x = _matmul_residual(x, y, w_out)
    return x, cso[:, SUBLANES - hist:], sso.reshape(bsz, SSD_HEADS, SSD_HEAD_DIM, SSD_STATE)


def _run(x3, s5_re, s5_im, conv_st, ssm_st, p):
    bsz, tlen, d = x3.shape
    x = x3.reshape(bsz * tlen, d)
    s5_re_out, s5_im_out, conv_out, ssm_out = [], [], [], []
    for i in range(DEPTH):
        j = i // 2
        if i % 2 == 0:
            x, hr, hi = _s5_layer(x, bsz, tlen, s5_re[j], s5_im[j], p["norm_mix"][i], p["s5"][j],
                                  p["s5_d"][j], p["s5_w_glu"][j])
            s5_re_out.append(hr)
            s5_im_out.append(hi)
        else:
            x, c, s = _ssd_layer(x, bsz, tlen, conv_st[j], ssm_st[j], p["norm_mix"][i], p["ssd_w_in"][j],
                                 p["ssd_conv_w"][j], p["ssd_conv_b"][j], p["ssd_dt_bias"][j],
                                 p["ssd_a_log"][j], p["ssd_d"][j], p["ssd_norm"][j], p["ssd_w_out"][j])
            conv_out.append(c)
            ssm_out.append(s)
        x = _mlp(x, p["norm_mlp"][i], p["mlp_w1"][i], p["mlp_w2"][i], p["norm_final"],
                 final_norm=(i == DEPTH - 1))
    return (x.reshape(bsz, tlen, d), jnp.stack(s5_re_out), jnp.stack(s5_im_out),
            jnp.stack(conv_out), jnp.stack(ssm_out))


def kernel(x_prompt, x_sample, state_s5_re, state_s5_im, state_ssd_conv, state_ssd_ssm, norm_mix, norm_mlp, norm_final, s5_lam_re, s5_lam_im, s5_log_dt, s5_b_re, s5_b_im, s5_c_re, s5_c_im, s5_d, s5_w_glu, ssd_w_in, ssd_conv_w, ssd_conv_b, ssd_dt_bias, ssd_a_log, ssd_d, ssd_norm, ssd_w_out, mlp_w1, mlp_w2):
    n_s5 = s5_lam_re.shape[0]
    n_ssd = ssd_w_in.shape[0]
    p = dict(
        norm_mix=norm_mix, norm_mlp=norm_mlp, norm_final=norm_final,
        s5=[_s5_params(s5_lam_re[j], s5_lam_im[j], s5_log_dt[j], s5_b_re[j], s5_b_im[j],
                       s5_c_re[j], s5_c_im[j]) for j in range(n_s5)],
        s5_d=s5_d, s5_w_glu=s5_w_glu.astype(BF16),
        ssd_w_in=[_ssd_in_weight(ssd_w_in[j]) for j in range(n_ssd)],
        ssd_conv_w=ssd_conv_w, ssd_conv_b=ssd_conv_b, ssd_dt_bias=ssd_dt_bias, ssd_a_log=ssd_a_log,
        ssd_d=ssd_d, ssd_norm=ssd_norm, ssd_w_out=ssd_w_out.astype(BF16),
        mlp_w1=mlp_w1.astype(BF16), mlp_w2=mlp_w2.astype(BF16),
    )
    bp = x_prompt.shape[0]
    zeros = lambda shape: jnp.zeros(shape, F32)
    y_p, s5_re_p, s5_im_p, conv_p, ssm_p = _run(
        x_prompt,
        zeros((n_s5, bp, S5_GROUPS, S5_STATE)), zeros((n_s5, bp, S5_GROUPS, S5_STATE)),
        zeros((n_ssd, bp, SSD_CONV - 1, SSD_CONV_DIM)),
        zeros((n_ssd, bp, SSD_HEADS, SSD_HEAD_DIM, SSD_STATE)), p)
    y_s, s5_re_s, s5_im_s, conv_s, ssm_s = _run(
        x_sample, state_s5_re, state_s5_im, state_ssd_conv, state_ssd_ssm, p)
    return (y_p, y_s, s5_re_p, s5_im_p, conv_p, ssm_p, s5_re_s, s5_im_s, conv_s, ssm_s)
```

```python
import functools

import jax
import jax.numpy as jnp
from jax import lax
from jax.experimental import pallas as pl
from jax.experimental.pallas import tpu as pltpu

F32 = jnp.float32
BF16 = jnp.bfloat16
HIGHEST = lax.Precision.HIGHEST

D_MODEL = 2048
DEPTH = 4
RMS_EPS = 1e-5
S5_GROUP = 16
S5_GROUPS = D_MODEL // S5_GROUP
S5_STATE = 64
S5_SLABS = 8
S5_SLAB_CH = D_MODEL // S5_SLABS
S5_SLAB_ST = (S5_GROUPS // S5_SLABS) * S5_STATE
SSD_INNER = 2 * D_MODEL
SSD_HEAD_DIM = 64
SSD_HEADS = SSD_INNER // SSD_HEAD_DIM
SSD_GROUPS = 8
SSD_STATE = 128
SSD_CONV = 4
SSD_CONV_DIM = SSD_INNER + 2 * SSD_GROUPS * SSD_STATE
SSD_GROUP_CH = SSD_INNER // SSD_GROUPS
LANES = 128
SUBLANES = 8
SSD_PROJ_PAD = 10752
D_FF = 4 * D_MODEL

VMEM_LIMIT = 56 * 1024 * 1024

NT_DIMS = (((1,), (1,)), ((), ()))


def _cparams(n_axes):
    return pltpu.CompilerParams(
        dimension_semantics=("arbitrary",) * n_axes, vmem_limit_bytes=VMEM_LIMIT)


def _rms(x, g):
    ms = jnp.mean(x * x, axis=-1, keepdims=True)
    return x * lax.rsqrt(ms + RMS_EPS) * g


def _row_tile(m, want):
    t = min(m, want)
    assert m % t == 0, (m, t)
    return t


def _norm_matmul_kernel(x_ref, g_ref, w_ref, o_ref, u_ref):
    @pl.when(pl.program_id(1) == 0)
    def _():
        u_ref[...] = _rms(x_ref[...], g_ref[...]).astype(BF16)

    o_ref[...] = jnp.dot(u_ref[...], w_ref[...], preferred_element_type=F32)


def _norm_matmul(x, g, w, tm=1024, tn=512):
    m, d = x.shape
    n = w.shape[1]
    tm = _row_tile(m, tm)
    return pl.pallas_call(
        _norm_matmul_kernel,
        grid=(m // tm, n // tn),
        in_specs=[pl.BlockSpec((tm, d), lambda i, j: (i, 0)),
                  pl.BlockSpec((1, d), lambda i, j: (0, 0)),
                  pl.BlockSpec((d, tn), lambda i, j: (0, j))],
        out_specs=pl.BlockSpec((tm, tn), lambda i, j: (i, j)),
        out_shape=jax.ShapeDtypeStruct((m, n), F32),
        scratch_shapes=[pltpu.VMEM((tm, d), BF16)],
        compiler_params=_cparams(2),
        name="norm_matmul",
    )(x, g.reshape(1, d), w)


def _mlp_kernel(x_ref, g_ref, w1_ref, w2_ref, gf_ref, o_ref, u_ref, *, nj, final_norm):
    j = pl.program_id(1)

    @pl.when(j == 0)
    def _():
        x = x_ref[...]
        u_ref[...] = _rms(x, g_ref[...]).astype(BF16)
        o_ref[...] = x

    a = jnp.dot(u_ref[...], w1_ref[...], preferred_element_type=F32)
    a = jnp.maximum(a, 0.0)
    a = (a * a).astype(BF16)
    o_ref[...] += jnp.dot(a, w2_ref[...], preferred_element_type=F32)

    if final_norm:
        @pl.when(j == nj - 1)
        def _():
            o_ref[...] = _rms(o_ref[...], gf_ref[...])


def _mlp(x, g, w1, w2, gf, final_norm, tm=512, tf=1024):
    m, d = x.shape
    f = w1.shape[1]
    tm = _row_tile(m, tm)
    nj = f // tf
    return pl.pallas_call(
        functools.partial(_mlp_kernel, nj=nj, final_norm=final_norm),
        grid=(m // tm, nj),
        in_specs=[pl.BlockSpec((tm, d), lambda i, j: (i, 0)),
                  pl.BlockSpec((1, d), lambda i, j: (0, 0)),
                  pl.BlockSpec((d, tf), lambda i, j: (0, j)),
                  pl.BlockSpec((tf, d), lambda i, j: (j, 0)),
                  pl.BlockSpec((1, d), lambda i, j: (0, 0))],
        out_specs=pl.BlockSpec((tm, d), lambda i, j: (i, 0)),
        out_shape=jax.ShapeDtypeStruct((m, d), F32),
        scratch_shapes=[pltpu.VMEM((tm, d), BF16)],
        compiler_params=_cparams(2),
        name="mlp",
    )(x, g.reshape(1, d), w1, w2, gf.reshape(1, d))


def _matmul_residual_kernel(x_ref, a_ref, w_ref, o_ref):
    o_ref[...] = x_ref[...] + jnp.dot(a_ref[...], w_ref[...], preferred_element_type=F32)


def _resident(shape):
    return pl.BlockSpec(shape, lambda i: (0,) * len(shape), pipeline_mode=pl.Buffered(1))


def _matmul_residual(x, a, w, tm=512):
    m, n = x.shape
    kdim = a.shape[1]
    tm = _row_tile(m, tm)
    return pl.pallas_call(
        _matmul_residual_kernel,
        grid=(m // tm,),
        in_specs=[pl.BlockSpec((tm, n), lambda i: (i, 0)),
                  pl.BlockSpec((tm, kdim), lambda i: (i, 0)),
                  _resident((kdim, n))],
        out_specs=pl.BlockSpec((tm, n), lambda i: (i, 0)),
        out_shape=jax.ShapeDtypeStruct((m, n), F32),
        compiler_params=_cparams(1),
        name="matmul_residual",
    )(x, a, w)


def _glu_residual_kernel(x_ref, a_ref, w_ref, o_ref):
    a = a_ref[...]
    n = o_ref.shape[1]
    ga = jnp.dot(a, w_ref[:, :n], preferred_element_type=F32)
    gb = jnp.dot(a, w_ref[:, n:], preferred_element_type=F32)
    o_ref[...] = x_ref[...] + ga * jax.nn.sigmoid(gb)


def _glu_residual(x, a, w, tm=512):
    m, n = x.shape
    kdim = a.shape[1]
    tm = _row_tile(m, tm)
    return pl.pallas_call(
        _glu_residual_kernel,
        grid=(m // tm,),
        in_specs=[pl.BlockSpec((tm, n), lambda i: (i, 0)),
                  pl.BlockSpec((tm, kdim), lambda i: (i, 0)),
                  _resident((kdim, 2 * n))],
        out_specs=pl.BlockSpec((tm, n), lambda i: (i, 0)),
        out_shape=jax.ShapeDtypeStruct((m, n), F32),
        compiler_params=_cparams(1),
        name="glu_residual",
    )(x, a, w)


S5_SCAN_CHUNKS = 2


S5_STEP_SLABS = 2


def _s5_kernel(x_ref, gn_ref, wb_ref, wc_ref, lam_ref, d_ref, h0_ref, g_ref, hf_ref,
               u_ref, hbuf_ref, hs_ref, *, bsz, tt, nk):
    k = pl.program_id(0)
    sp = pl.program_id(1)
    ns = S5_STEP_SLABS
    wch = ns * S5_SLAB_CH

    @pl.when((k == 0) & (sp == 0))
    def _():
        hs_ref[...] = h0_ref[...]

    @pl.when(sp == 0)
    def _():
        un = _rms(x_ref[...], gn_ref[...])
        for j in range(D_MODEL // wch):
            u_ref[j] = un[:, j * wch:(j + 1) * wch]

    u = u_ref[sp]
    for i in range(ns):
        ui = u[:, i * S5_SLAB_CH:(i + 1) * S5_SLAB_CH].astype(BF16)
        hbuf_ref[i] = jnp.dot(ui, wb_ref[sp * ns + i], preferred_element_type=F32)

    cw = S5_SLAB_ST // S5_SCAN_CHUNKS
    for i in range(ns):
        slab = sp * ns + i
        for q in range(S5_SCAN_CHUNKS):
            re = slice(q * cw, (q + 1) * cw)
            im = slice(S5_SLAB_ST + q * cw, S5_SLAB_ST + (q + 1) * cw)
            lr = lam_ref[slab, 0:1, re]
            li = lam_ref[slab, 1:2, re]

            def step(t, carry, i=i, re=re, im=im, lr=lr, li=li):
                hr, hi = carry
                rows = pl.ds(pl.multiple_of(t * bsz, bsz), bsz)
                nr = lr * hr - li * hi + hbuf_ref[i, rows, re]
                ni = lr * hi + li * hr + hbuf_ref[i, rows, im]
                hbuf_ref[i, rows, re] = nr
                hbuf_ref[i, rows, im] = ni
                return nr, ni

            hr, hi = lax.fori_loop(0, tt, step, (hs_ref[slab, :, re], hs_ref[slab, :, im]), unroll=4)
            hs_ref[slab, :, re] = hr
            hs_ref[slab, :, im] = hi

    d = d_ref[sp]
    for i in range(ns):
        cs = slice(i * S5_SLAB_CH, (i + 1) * S5_SLAB_CH)
        y = jnp.dot(hbuf_ref[i].astype(BF16), wc_ref[sp * ns + i], preferred_element_type=F32)
        y = y + d[:, cs] * u[:, cs]
        g_ref[:, cs] = jax.nn.gelu(y).astype(BF16)

    @pl.when((k == nk - 1) & (sp == S5_SLABS // ns - 1))
    def _():
        hf_ref[...] = hs_ref[...]


def _s5_core(xt, norm_g, wb, wc, lam, d, h0, bsz, tlen):
    tt = min(tlen, 512 // bsz)
    rows = tt * bsz
    nk = tlen // tt
    ns = S5_STEP_SLABS
    nsp = S5_SLABS // ns
    full = lambda shape: pl.BlockSpec(shape, lambda k, s: (0,) * len(shape),
                                      pipeline_mode=pl.Buffered(1))
    return pl.pallas_call(
        functools.partial(_s5_kernel, bsz=bsz, tt=tt, nk=nk),
        grid=(nk, nsp),
        in_specs=[pl.BlockSpec((rows, D_MODEL), lambda k, s: (k, 0)),
                  full((1, D_MODEL)),
                  full((S5_SLABS, S5_SLAB_CH, 2 * S5_SLAB_ST)),
                  full((S5_SLABS, 2 * S5_SLAB_ST, S5_SLAB_CH)),
                  full((S5_SLABS, 2, S5_SLAB_ST)),
                  full((nsp, 1, ns * S5_SLAB_CH)),
                  full((S5_SLABS, bsz, 2 * S5_SLAB_ST))],
        out_specs=[pl.BlockSpec((rows, ns * S5_SLAB_CH), lambda k, s: (k, s)),
                   pl.BlockSpec((S5_SLABS, bsz, 2 * S5_SLAB_ST), lambda k, s: (0, 0, 0))],
        out_shape=[jax.ShapeDtypeStruct((tlen * bsz, D_MODEL), BF16),
                   jax.ShapeDtypeStruct((S5_SLABS, bsz, 2 * S5_SLAB_ST), F32)],
        scratch_shapes=[pltpu.VMEM((nsp, rows, ns * S5_SLAB_CH), F32),
                        pltpu.VMEM((ns, rows, 2 * S5_SLAB_ST), F32),
                        pltpu.VMEM((S5_SLABS, bsz, 2 * S5_SLAB_ST), F32)],
        compiler_params=_cparams(2),
        name="s5_core",
    )(xt, norm_g.reshape(1, D_MODEL), wb, wc, lam, d.reshape(nsp, 1, ns * S5_SLAB_CH), h0)


def _s5_params(lam_re, lam_im, log_dt, b_re, b_im, c_re, c_im):
    dt = jnp.exp(log_dt)[:, None]
    er = jnp.exp(lam_re * dt)
    lbr = er * jnp.cos(lam_im * dt)
    lbi = er * jnp.sin(lam_im * dt)
    den = lam_re * lam_re + lam_im * lam_im
    qr = ((lbr - 1.0) * lam_re + lbi * lam_im) / den
    qi = (lbi * lam_re - (lbr - 1.0) * lam_im) / den
    bbr = qr[..., None] * b_re - qi[..., None] * b_im
    bbi = qr[..., None] * b_im + qi[..., None] * b_re
    gl = S5_GROUPS // S5_SLABS
    eye = jnp.eye(gl, dtype=F32)

    def pack_b(bb):
        bb = bb.reshape(S5_SLABS, gl, S5_STATE, S5_GROUP)
        w = jnp.einsum("sgpc,gh->sgchp", bb, eye)
        return w.reshape(S5_SLABS, S5_SLAB_CH, S5_SLAB_ST)

    def pack_c(cc):
        cc = cc.reshape(S5_SLABS, gl, S5_GROUP, S5_STATE)
        w = jnp.einsum("sgcp,gh->sgphc", cc, eye)
        return w.reshape(S5_SLABS, S5_SLAB_ST, S5_SLAB_CH)

    wb = jnp.concatenate([pack_b(bbr), pack_b(bbi)], axis=2).astype(BF16)
    wc = jnp.concatenate([pack_c(c_re), pack_c(-c_im)], axis=1).astype(BF16)
    lam = jnp.stack([lbr.reshape(S5_SLABS, S5_SLAB_ST), lbi.reshape(S5_SLABS, S5_SLAB_ST)], axis=1)
    return wb, wc, lam


def _s5_state_in(h):
    bsz = h.shape[0]
    return h.reshape(bsz, S5_SLABS, S5_SLAB_ST).transpose(1, 0, 2)


def _s5_state_out(h):
    bsz = h.shape[1]
    return h.transpose(1, 0, 2).reshape(bsz, S5_GROUPS, S5_STATE)


def _s5_layer(x, bsz, tlen, h_re, h_im, norm_g, prm, d_skip, w_glu):
    wb, wc, lam = prm
    xt = x.reshape(bsz, tlen, D_MODEL).transpose(1, 0, 2).reshape(tlen * bsz, D_MODEL)
    h0 = jnp.concatenate([_s5_state_in(h_re), _s5_state_in(h_im)], axis=2)
    gt, hf = _s5_core(xt, norm_g, wb, wc, lam, d_skip, h0, bsz, tlen)
    g = gt.reshape(tlen, bsz, D_MODEL).transpose(1, 0, 2).reshape(bsz * tlen, D_MODEL)
    x = _glu_residual(x, g, w_glu)
    return x, _s5_state_out(hf[:, :, :S5_SLAB_ST]), _s5_state_out(hf[:, :, S5_SLAB_ST:])


def _softplus(x):
    return jnp.maximum(x, 0.0) + jnp.log1p(jnp.exp(-jnp.abs(x)))


def _lane_bcast(v, col, width):
    return jnp.broadcast_to(v[:, col:col + 1], (v.shape[0], width))


def _ssd_kernel(xbc_ref, z0_ref, z1_ref, dt_ref, cst_ref, sst_ref, cw_ref, cb_ref, dtb_ref,
                alog_ref, dsk_ref, nw_ref, y_ref, cso_ref, sso_ref,
                xp_ref, st_ref, xs_ref, yb_ref, *, clen, nk):
    k = pl.program_id(1)
    hdr = SUBLANES

    @pl.when(k == 0)
    def _():
        xp_ref[0:hdr, :] = cst_ref[0]
        st_ref[...] = sst_ref[0]

    xp_ref[hdr:hdr + clen, :] = xbc_ref[...]
    xc = cb_ref[...]
    for kk in range(SSD_CONV):
        r0 = hdr - (SSD_CONV - 1) + kk
        xc = xc + cw_ref[kk:kk + 1, :] * xp_ref[r0:r0 + clen, :]
    xc = xc * jax.nn.sigmoid(xc)
    xp_ref[0:hdr, :] = xp_ref[clen:clen + hdr, :]
    xs_ref[...] = xc[:, :SSD_INNER]
    bmat = xc[:, SSD_INNER:SSD_INNER + SSD_GROUPS * SSD_STATE].astype(BF16)
    cmat = xc[:, SSD_INNER + SSD_GROUPS * SSD_STATE:].astype(BF16)

    dt = _softplus(dt_ref[...] + dtb_ref[...])
    da = dt * (-jnp.exp(alog_ref[...]))
    ri = lax.broadcasted_iota(jnp.int32, (clen, clen), 0)
    ci = lax.broadcasted_iota(jnp.int32, (clen, clen), 1)
    causal = ri >= ci
    acum = jnp.dot(causal.astype(F32), da, precision=HIGHEST, preferred_element_type=F32)
    eye = (lax.broadcasted_iota(jnp.int32, (LANES, LANES), 0)
           == lax.broadcasted_iota(jnp.int32, (LANES, LANES), 1)).astype(F32)
    acum_t = lax.dot_general(eye, acum, NT_DIMS, precision=HIGHEST, preferred_element_type=F32)
    dt_t = lax.dot_general(eye, dt, NT_DIMS, precision=HIGHEST, preferred_element_type=F32)
    eacum = jnp.exp(acum)
    tailw = jnp.exp(acum[clen - 1:clen, :] - acum) * dt
    sdec = jnp.exp(acum_t[:, clen - 1:clen])

    lo_lane = lax.broadcasted_iota(jnp.int32, (clen, LANES), 1) < SSD_HEAD_DIM
    lo_row = lax.broadcasted_iota(jnp.int32, (LANES, SSD_STATE), 0) < SSD_HEAD_DIM
    neg_inf = jnp.float32(-jnp.inf)

    for g in range(SSD_GROUPS):
        gs = slice(g * SSD_STATE, (g + 1) * SSD_STATE)
        cg = cmat[:, gs]
        bg = bmat[:, gs]
        cbm = lax.dot_general(cg, bg, NT_DIMS, preferred_element_type=F32)
        sg = st_ref[g * SSD_GROUP_CH:(g + 1) * SSD_GROUP_CH, :].astype(BF16)
        chs = lax.dot_general(cg, sg, NT_DIMS, preferred_element_type=F32)
        for j in range(SSD_GROUP_CH // LANES):
            h0 = (g * SSD_GROUP_CH + j * LANES) // SSD_HEAD_DIM
            h1 = h0 + 1
            cs = slice(g * SSD_GROUP_CH + j * LANES, g * SSD_GROUP_CH + (j + 1) * LANES)
            xf = xs_ref[:, cs]
            xb = xf.astype(BF16)
            ys = []
            for h in (h0, h1):
                diff = _lane_bcast(acum, h, clen) - acum_t[h:h + 1, :]
                w = cbm * jnp.exp(jnp.where(causal, diff, neg_inf)) * dt_t[h:h + 1, :]
                ys.append(jnp.dot(w.astype(BF16), xb, preferred_element_type=F32))
            y = jnp.where(lo_lane, ys[0], ys[1])
            ea = jnp.where(lo_lane, _lane_bcast(eacum, h0, LANES), _lane_bcast(eacum, h1, LANES))
            y = y + chs[:, j * LANES:(j + 1) * LANES] * ea + dsk_ref[:, cs] * xf
            yb_ref[:, cs] = y
            tw = jnp.where(lo_lane, _lane_bcast(tailw, h0, LANES), _lane_bcast(tailw, h1, LANES))
            txb = (xf * tw).astype(BF16)
            ds = lax.dot_general(txb, bg, (((0,), (0,)), ((), ())), preferred_element_type=F32)
            dec = jnp.where(lo_row,
                            jnp.broadcast_to(sdec[h0:h0 + 1, :], (LANES, SSD_STATE)),
                            jnp.broadcast_to(sdec[h1:h1 + 1, :], (LANES, SSD_STATE)))
            st_ref[cs, :] = st_ref[cs, :] * dec + ds

    half = SSD_INNER // 2
    for hf, z_ref in enumerate((z0_ref, z1_ref)):
        z = z_ref[...]
        yh = yb_ref[:, hf * half:(hf + 1) * half] * (z * jax.nn.sigmoid(z))
        for gg in range(half // SSD_GROUP_CH):
            yg = yh[:, gg * SSD_GROUP_CH:(gg + 1) * SSD_GROUP_CH]
            ms = jnp.mean(yg * yg, axis=-1, keepdims=True)
            cs = slice(hf * half + gg * SSD_GROUP_CH, hf * half + (gg + 1) * SSD_GROUP_CH)
            y_ref[:, cs] = (yg * lax.rsqrt(ms + RMS_EPS) * nw_ref[:, cs]).astype(BF16)

    @pl.when(k == nk - 1)
    def _():
        cso_ref[0] = xp_ref[0:hdr, :]
        sso_ref[0] = st_ref[...]


def _ssd_core(proj, cst, sst, conv_w, conv_b, dt_bias, a_log, d_exp, norm_w, bsz, tlen):
    clen = min(tlen, 128)
    nk = tlen // clen
    zblk = SSD_INNER // 2
    z_off = SSD_CONV_DIM // zblk
    dt_off = (SSD_CONV_DIM + SSD_INNER) // LANES
    row = lambda b, k: b * nk + k
    pad = lambda v: jnp.pad(v, (0, LANES - SSD_HEADS)).reshape(1, LANES)
    return pl.pallas_call(
        functools.partial(_ssd_kernel, clen=clen, nk=nk),
        grid=(bsz, nk),
        in_specs=[pl.BlockSpec((clen, SSD_CONV_DIM), lambda b, k: (row(b, k), 0)),
                  pl.BlockSpec((clen, zblk), lambda b, k: (row(b, k), z_off)),
                  pl.BlockSpec((clen, zblk), lambda b, k: (row(b, k), z_off + 1)),
                  pl.BlockSpec((clen, LANES), lambda b, k: (row(b, k), dt_off)),
                  pl.BlockSpec((1, SUBLANES, SSD_CONV_DIM), lambda b, k: (b, 0, 0)),
                  pl.BlockSpec((1, SSD_INNER, SSD_STATE), lambda b, k: (b, 0, 0)),
                  pl.BlockSpec((SSD_CONV, SSD_CONV_DIM), lambda b, k: (0, 0)),
                  pl.BlockSpec((1, SSD_CONV_DIM), lambda b, k: (0, 0)),
                  pl.BlockSpec((1, LANES), lambda b, k: (0, 0)),
                  pl.BlockSpec((1, LANES), lambda b, k: (0, 0)),
                  pl.BlockSpec((1, SSD_INNER), lambda b, k: (0, 0)),
                  pl.BlockSpec((1, SSD_INNER), lambda b, k: (0, 0))],
        out_specs=[pl.BlockSpec((clen, SSD_INNER), lambda b, k: (row(b, k), 0)),
                   pl.BlockSpec((1, SUBLANES, SSD_CONV_DIM), lambda b, k: (b, 0, 0)),
                   pl.BlockSpec((1, SSD_INNER, SSD_STATE), lambda b, k: (b, 0, 0))],
        out_shape=[jax.ShapeDtypeStruct((bsz * tlen, SSD_INNER), BF16),
                   jax.ShapeDtypeStruct((bsz, SUBLANES, SSD_CONV_DIM), F32),
                   jax.ShapeDtypeStruct((bsz, SSD_INNER, SSD_STATE), F32)],
        scratch_shapes=[pltpu.VMEM((clen + SUBLANES, SSD_CONV_DIM), F32),
                        pltpu.VMEM((SSD_INNER, SSD_STATE), F32),
                        pltpu.VMEM((clen, SSD_INNER), F32),
                        pltpu.VMEM((clen, SSD_INNER), F32)],
        compiler_params=_cparams(2),
        name="ssd_core",
    )(proj, proj, proj, proj, cst, sst, conv_w, conv_b.reshape(1, SSD_CONV_DIM),
      pad(dt_bias), pad(a_log), d_exp.reshape(1, SSD_INNER), norm_w.reshape(1, SSD_INNER))


def _ssd_in_weight(w_in):
    d = w_in.shape[0]
    z = w_in[:, :SSD_INNER]
    xbc = w_in[:, SSD_INNER:SSD_INNER + SSD_CONV_DIM]
    dt = w_in[:, SSD_INNER + SSD_CONV_DIM:]
    padw = SSD_PROJ_PAD - (SSD_CONV_DIM + SSD_INNER + SSD_HEADS)
    return jnp.concatenate([xbc, z, dt, jnp.zeros((d, padw), w_in.dtype)], axis=1).astype(BF16)


def _ssd_layer(x, bsz, tlen, conv_st, ssm_st, norm_g, w_in_p, conv_w, conv_b, dt_bias, a_log,
               d_skip, norm_w, w_out):
    proj = _norm_matmul(x, norm_g, w_in_p)
    hist = SSD_CONV - 1
    cst = jnp.pad(conv_st, ((0, 0), (SUBLANES - hist, 0), (0, 0)))
    sst = ssm_st.reshape(bsz, SSD_INNER, SSD_STATE)
    d_exp = jnp.repeat(d_skip, SSD_HEAD_DIM)
    y, cso, sso = _ssd_core(proj, cst, sst, conv_w, conv_b, dt_bias, a_log, d_exp, norm_w, bsz, tlen)
    x = _matmul_residual(x, y, w_out)
    return x, cso[:, SUBLANES - hist:], sso.reshape(bsz, SSD_HEADS, SSD_HEAD_DIM, SSD_STATE)


def _run(x3, s5_re, s5_im, conv_st, ssm_st, p):
    bsz, tlen, d = x3.shape
    x = x3.reshape(bsz * tlen, d)
    s5_re_out, s5_im_out, conv_out, ssm_out = [], [], [], []
    for i in range(DEPTH):
        j = i // 2
        if i % 2 == 0:
            x, hr, hi = _s5_layer(x, bsz, tlen, s5_re[j], s5_im[j], p["norm_mix"][i], p["s5"][j],
                                  p["s5_d"][j], p["s5_w_glu"][j])
            s5_re_out.append(hr)
            s5_im_out.append(hi)
        else:
            x, c, s = _ssd_layer(x, bsz, tlen, conv_st[j], ssm_st[j], p["norm_mix"][i], p["ssd_w_in"][j],
                                 p["ssd_conv_w"][j], p["ssd_conv_b"][j], p["ssd_dt_bias"][j],
                                 p["ssd_a_log"][j], p["ssd_d"][j], p["ssd_norm"][j], p["ssd_w_out"][j])
            conv_out.append(c)
            ssm_out.append(s)
        x = _mlp(x, p["norm_mlp"][i], p["mlp_w1"][i], p["mlp_w2"][i], p["norm_final"],
                 final_norm=(i == DEPTH - 1))
    return (x.reshape(bsz, tlen, d), jnp.stack(s5_re_out), jnp.stack(s5_im_out),
            jnp.stack(conv_out), jnp.stack(ssm_out))


def kernel(x_prompt, x_sample, state_s5_re, state_s5_im, state_ssd_conv, state_ssd_ssm, norm_mix, norm_mlp, norm_final, s5_lam_re, s5_lam_im, s5_log_dt, s5_b_re, s5_b_im, s5_c_re, s5_c_im, s5_d, s5_w_glu, ssd_w_in, ssd_conv_w, ssd_conv_b, ssd_dt_bias, ssd_a_log, ssd_d, ssd_norm, ssd_w_out, mlp_w1, mlp_w2):
    n_s5 = s5_lam_re.shape[0]
    n_ssd = ssd_w_in.shape[0]
    p = dict(
        norm_mix=norm_mix, norm_mlp=norm_mlp, norm_final=norm_final,
        s5=[_s5_params(s5_lam_re[j], s5_lam_im[j], s5_log_dt[j], s5_b_re[j], s5_b_im[j],
                       s5_c_re[j], s5_c_im[j]) for j in range(n_s5)],
        s5_d=s5_d, s5_w_glu=s5_w_glu.astype(BF16),
        ssd_w_in=[_ssd_in_weight(ssd_w_in[j]) for j in range(n_ssd)],
        ssd_conv_w=ssd_conv_w, ssd_conv_b=ssd_conv_b, ssd_dt_bias=ssd_dt_bias, ssd_a_log=ssd_a_log,
        ssd_d=ssd_d, ssd_norm=ssd_norm, ssd_w_out=ssd_w_out.astype(BF16),
        mlp_w1=mlp_w1.astype(BF16), mlp_w2=mlp_w2.astype(BF16),
    )
    bp = x_prompt.shape[0]
    zeros = lambda shape: jnp.zeros(shape, F32)
    y_p, s5_re_p, s5_im_p, conv_p, ssm_p = _run(
        x_prompt,
        zeros((n_s5, bp, S5_GROUPS, S5_STATE)), zeros((n_s5, bp, S5_GROUPS, S5_STATE)),
        zeros((n_ssd, bp, SSD_CONV - 1, SSD_CONV_DIM)),
        zeros((n_ssd, bp, SSD_HEADS, SSD_HEAD_DIM, SSD_STATE)), p)
    y_s, s5_re_s, s5_im_s, conv_s, ssm_s = _run(
        x_sample, state_s5_re, state_s5_im, state_ssd_conv, state_ssd_ssm, p)
    return (y_p, y_s, s5_re_p, s5_im_p, conv_p, ssm_p, s5_re_s, s5_im_s, conv_s, ssm_s)
```

```python
import functools

import jax
import jax.numpy as jnp
from jax import lax
from jax.experimental import pallas as pl
from jax.experimental.pallas import tpu as pltpu

F32 = jnp.float32
BF16 = jnp.bfloat16
HIGHEST = lax.Precision.HIGHEST

D_MODEL = 2048
DEPTH = 4
RMS_EPS = 1e-5
S5_GROUP = 16
S5_GROUPS = D_MODEL // S5_GROUP
S5_STATE = 64
S5_SLABS = 8
S5_SLAB_CH = D_MODEL // S5_SLABS
S5_SLAB_ST = (S5_GROUPS // S5_SLABS) * S5_STATE
SSD_INNER = 2 * D_MODEL
SSD_HEAD_DIM = 64
SSD_HEADS = SSD_INNER // SSD_HEAD_DIM
SSD_GROUPS = 8
SSD_STATE = 128
SSD_CONV = 4
SSD_CONV_DIM = SSD_INNER + 2 * SSD_GROUPS * SSD_STATE
SSD_GROUP_CH = SSD_INNER // SSD_GROUPS
LANES = 128
SUBLANES = 8
SSD_PROJ_PAD = 10752
D_FF = 4 * D_MODEL

VMEM_LIMIT = 56 * 1024 * 1024

NT_DIMS = (((1,), (1,)), ((), ()))


def _cparams(n_axes):
    return pltpu.CompilerParams(
        dimension_semantics=("arbitrary",) * n_axes, vmem_limit_bytes=VMEM_LIMIT)


def _rms(x, g):
    ms = jnp.mean(x * x, axis=-1, keepdims=True)
    return x * lax.rsqrt(ms + RMS_EPS) * g


def _silu(x):
    return 0.5 * x * (1.0 + jnp.tanh(0.5 * x))


def _row_tile(m, want):
    t = min(m, want)
    assert m % t == 0, (m, t)
    return t


def _norm_matmul_kernel(x_ref, g_ref, w_ref, o_ref, u_ref):
    @pl.when(pl.program_id(1) == 0)
    def _():
        u_ref[...] = _rms(x_ref[...], g_ref[...]).astype(BF16)

    o_ref[...] = jnp.dot(u_ref[...], w_ref[...], preferred_element_type=F32)


def _norm_matmul(x, g, w, tm=1024, tn=768):
    m, d = x.shape
    n = w.shape[1]
    tm = _row_tile(m, tm)
    return pl.pallas_call(
        _norm_matmul_kernel,
        grid=(m // tm, n // tn),
        in_specs=[pl.BlockSpec((tm, d), lambda i, j: (i, 0)),
                  pl.BlockSpec((1, d), lambda i, j: (0, 0)),
                  pl.BlockSpec((d, tn), lambda i, j: (0, j))],
        out_specs=pl.BlockSpec((tm, tn), lambda i, j: (i, j)),
        out_shape=jax.ShapeDtypeStruct((m, n), F32),
        scratch_shapes=[pltpu.VMEM((tm, d), BF16)],
        compiler_params=_cparams(2),
        name="norm_matmul",
    )(x, g.reshape(1, d), w)


def _causal_conv_silu(stage_ref, cw_ref, cb_ref, rows):
    xc = cb_ref[...]
    for kk in range(SSD_CONV):
        r0 = SUBLANES - (SSD_CONV - 1) + kk
        xc = xc + cw_ref[kk:kk + 1, :] * stage_ref[r0:r0 + rows, :]
    return _silu(xc)


def _mlp_kernel(x_ref, g_ref, w1_ref, w2_ref, gf_ref, o_ref, u_ref, *, nj, final_norm):
    j = pl.program_id(1)

    @pl.when(j == 0)
    def _():
        x = x_ref[...]
        u_ref[...] = _rms(x, g_ref[...]).astype(BF16)
        o_ref[...] = x

    a = jnp.dot(u_ref[...], w1_ref[...], preferred_element_type=F32)
    a = jnp.maximum(a, 0.0)
    a = (a * a).astype(BF16)
    o_ref[...] += jnp.dot(a, w2_ref[...], preferred_element_type=F32)

    if final_norm:
        @pl.when(j == nj - 1)
        def _():
            o_ref[...] = _rms(o_ref[...], gf_ref[...])


def _mlp(x, g, w1, w2, gf, final_norm, tm=1024, tf=512):
    m, d = x.shape
    f = w1.shape[1]
    tm = _row_tile(m, tm)
    nj = f // tf
    return pl.pallas_call(
        functools.partial(_mlp_kernel, nj=nj, final_norm=final_norm),
        grid=(m // tm, nj),
        in_specs=[pl.BlockSpec((tm, d), lambda i, j: (i, 0)),
                  pl.BlockSpec((1, d), lambda i, j: (0, 0)),
                  pl.BlockSpec((d, tf), lambda i, j: (0, j)),
                  pl.BlockSpec((tf, d), lambda i, j: (j, 0)),
                  pl.BlockSpec((1, d), lambda i, j: (0, 0))],
        out_specs=pl.BlockSpec((tm, d), lambda i, j: (i, 0)),
        out_shape=jax.ShapeDtypeStruct((m, d), F32),
        scratch_shapes=[pltpu.VMEM((tm, d), BF16)],
        compiler_params=_cparams(2),
        name="mlp",
    )(x, g.reshape(1, d), w1, w2, gf.reshape(1, d))


def _matmul_residual_kernel(x_ref, a_ref, w_ref, o_ref):
    o_ref[...] = x_ref[...] + jnp.dot(a_ref[...], w_ref[...], preferred_element_type=F32)


def _resident(shape):
    return pl.BlockSpec(shape, lambda i: (0,) * len(shape), pipeline_mode=pl.Buffered(1))


def _matmul_residual(x, a, w, tm=512):
    m, n = x.shape
    kdim = a.shape[1]
    tm = _row_tile(m, tm)
    return pl.pallas_call(
        _matmul_residual_kernel,
        grid=(m // tm,),
        in_specs=[pl.BlockSpec((tm, n), lambda i: (i, 0)),
                  pl.BlockSpec((tm, kdim), lambda i: (i, 0)),
                  _resident((kdim, n))],
        out_specs=pl.BlockSpec((tm, n), lambda i: (i, 0)),
        out_shape=jax.ShapeDtypeStruct((m, n), F32),
        compiler_params=_cparams(1),
        name="matmul_residual",
    )(x, a, w)


def _glu_residual_kernel(x_ref, a_ref, w_ref, o_ref):
    a = a_ref[...]
    n = o_ref.shape[1]
    ga = jnp.dot(a, w_ref[:, :n], preferred_element_type=F32)
    gb = jnp.dot(a, w_ref[:, n:], preferred_element_type=F32)
    o_ref[...] = x_ref[...] + ga * jax.nn.sigmoid(gb)


def _glu_residual(x, a, w, tm=512):
    m, n = x.shape
    kdim = a.shape[1]
    tm = _row_tile(m, tm)
    return pl.pallas_call(
        _glu_residual_kernel,
        grid=(m // tm,),
        in_specs=[pl.BlockSpec((tm, n), lambda i: (i, 0)),
                  pl.BlockSpec((tm, kdim), lambda i: (i, 0)),
                  _resident((kdim, 2 * n))],
        out_specs=pl.BlockSpec((tm, n), lambda i: (i, 0)),
        out_shape=jax.ShapeDtypeStruct((m, n), F32),
        compiler_params=_cparams(1),
        name="glu_residual",
    )(x, a, w)


S5_SCAN_CHUNKS = 2


S5_STEP_SLABS = 2


def _s5_kernel(x_ref, gn_ref, wb_ref, wc_ref, lam_ref, d_ref, h0_ref, g_ref, hf_ref,
               u_ref, hbuf_ref, hs_ref, *, bsz, tt, nk):
    k = pl.program_id(0)
    sp = pl.program_id(1)
    ns = S5_STEP_SLABS
    wch = ns * S5_SLAB_CH

    @pl.when((k == 0) & (sp == 0))
    def _():
        hs_ref[...] = h0_ref[...]

    @pl.when(sp == 0)
    def _():
        un = _rms(x_ref[...], gn_ref[...])
        for j in range(D_MODEL // wch):
            u_ref[j] = un[:, j * wch:(j + 1) * wch]

    u = u_ref[sp]
    for i in range(ns):
        ui = u[:, i * S5_SLAB_CH:(i + 1) * S5_SLAB_CH].astype(BF16)
        hbuf_ref[i] = jnp.dot(ui, wb_ref[sp * ns + i], preferred_element_type=F32)

    cw = S5_SLAB_ST // S5_SCAN_CHUNKS
    for i in range(ns):
        slab = sp * ns + i
        for q in range(S5_SCAN_CHUNKS):
            re = slice(q * cw, (q + 1) * cw)
            im = slice(S5_SLAB_ST + q * cw, S5_SLAB_ST + (q + 1) * cw)
            lr = lam_ref[slab, 0:1, re]
            li = lam_ref[slab, 1:2, re]

            def step(t, carry, i=i, re=re, im=im, lr=lr, li=li):
                hr, hi = carry
                rows = pl.ds(pl.multiple_of(t * bsz, bsz), bsz)
                nr = lr * hr - li * hi + hbuf_ref[i, rows, re]
                ni = lr * hi + li * hr + hbuf_ref[i, rows, im]
                hbuf_ref[i, rows, re] = nr
                hbuf_ref[i, rows, im] = ni
                return nr, ni

            hr, hi = lax.fori_loop(0, tt, step, (hs_ref[slab, :, re], hs_ref[slab, :, im]), unroll=True)
            hs_ref[slab, :, re] = hr
            hs_ref[slab, :, im] = hi

    d = d_ref[sp]
    for i in range(ns):
        cs = slice(i * S5_SLAB_CH, (i + 1) * S5_SLAB_CH)
        y = jnp.dot(hbuf_ref[i].astype(BF16), wc_ref[sp * ns + i], preferred_element_type=F32)
        y = y + d[:, cs] * u[:, cs]
        g_ref[:, cs] = jax.nn.gelu(y).astype(BF16)

    @pl.when((k == nk - 1) & (sp == S5_SLABS // ns - 1))
    def _():
        hf_ref[...] = hs_ref[...]


def _s5_core(xt, norm_g, wb, wc, lam, d, h0, bsz, tlen):
    tt = min(tlen, 512 // bsz)
    rows = tt * bsz
    nk = tlen // tt
    ns = S5_STEP_SLABS
    nsp = S5_SLABS // ns
    full = lambda shape: pl.BlockSpec(shape, lambda k, s: (0,) * len(shape),
                                      pipeline_mode=pl.Buffered(1))
    return pl.pallas_call(
        functools.partial(_s5_kernel, bsz=bsz, tt=tt, nk=nk),
        grid=(nk, nsp),
        in_specs=[pl.BlockSpec((rows, D_MODEL), lambda k, s: (k, 0)),
                  full((1, D_MODEL)),
                  full((S5_SLABS, S5_SLAB_CH, 2 * S5_SLAB_ST)),
                  full((S5_SLABS, 2 * S5_SLAB_ST, S5_SLAB_CH)),
                  full((S5_SLABS, 2, S5_SLAB_ST)),
                  full((nsp, 1, ns * S5_SLAB_CH)),
                  full((S5_SLABS, bsz, 2 * S5_SLAB_ST))],
        out_specs=[pl.BlockSpec((rows, ns * S5_SLAB_CH), lambda k, s: (k, s)),
                   pl.BlockSpec((S5_SLABS, bsz, 2 * S5_SLAB_ST), lambda k, s: (0, 0, 0))],
        out_shape=[jax.ShapeDtypeStruct((tlen * bsz, D_MODEL), BF16),
                   jax.ShapeDtypeStruct((S5_SLABS, bsz, 2 * S5_SLAB_ST), F32)],
        scratch_shapes=[pltpu.VMEM((nsp, rows, ns * S5_SLAB_CH), F32),
                        pltpu.VMEM((ns, rows, 2 * S5_SLAB_ST), F32),
                        pltpu.VMEM((S5_SLABS, bsz, 2 * S5_SLAB_ST), F32)],
        compiler_params=_cparams(2),
        name="s5_core",
    )(xt, norm_g.reshape(1, D_MODEL), wb, wc, lam, d.reshape(nsp, 1, ns * S5_SLAB_CH), h0)


def _s5_params(lam_re, lam_im, log_dt, b_re, b_im, c_re, c_im):
    dt = jnp.exp(log_dt)[:, None]
    er = jnp.exp(lam_re * dt)
    lbr = er * jnp.cos(lam_im * dt)
    lbi = er * jnp.sin(lam_im * dt)
    den = lam_re * lam_re + lam_im * lam_im
    qr = ((lbr - 1.0) * lam_re + lbi * lam_im) / den
    qi = (lbi * lam_re - (lbr - 1.0) * lam_im) / den
    bbr = qr[..., None] * b_re - qi[..., None] * b_im
    bbi = qr[..., None] * b_im + qi[..., None] * b_re
    gl = S5_GROUPS // S5_SLABS
    eye = jnp.eye(gl, dtype=F32)

    def pack_b(bb):
        bb = bb.reshape(S5_SLABS, gl, S5_STATE, S5_GROUP)
        w = jnp.einsum("sgpc,gh->sgchp", bb, eye)
        return w.reshape(S5_SLABS, S5_SLAB_CH, S5_SLAB_ST)

    def pack_c(cc):
        cc = cc.reshape(S5_SLABS, gl, S5_GROUP, S5_STATE)
        w = jnp.einsum("sgcp,gh->sgphc", cc, eye)
        return w.reshape(S5_SLABS, S5_SLAB_ST, S5_SLAB_CH)

    wb = jnp.concatenate([pack_b(bbr), pack_b(bbi)], axis=2).astype(BF16)
    wc = jnp.concatenate([pack_c(c_re), pack_c(-c_im)], axis=1).astype(BF16)
    lam = jnp.stack([lbr.reshape(S5_SLABS, S5_SLAB_ST), lbi.reshape(S5_SLABS, S5_SLAB_ST)], axis=1)
    return wb, wc, lam


def _s5_state_in(h):
    bsz = h.shape[0]
    return h.reshape(bsz, S5_SLABS, S5_SLAB_ST).transpose(1, 0, 2)


def _s5_state_out(h):
    bsz = h.shape[1]
    return h.transpose(1, 0, 2).reshape(bsz, S5_GROUPS, S5_STATE)


def _s5_layer(x, bsz, tlen, h_re, h_im, norm_g, prm, d_skip, w_glu):
    wb, wc, lam = prm
    xt = x.reshape(bsz, tlen, D_MODEL).transpose(1, 0, 2).reshape(tlen * bsz, D_MODEL)
    h0 = jnp.concatenate([_s5_state_in(h_re), _s5_state_in(h_im)], axis=2)
    gt, hf = _s5_core(xt, norm_g, wb, wc, lam, d_skip, h0, bsz, tlen)
    g = gt.reshape(tlen, bsz, D_MODEL).transpose(1, 0, 2).reshape(bsz * tlen, D_MODEL)
    x = _glu_residual(x, g, w_glu)
    return x, _s5_state_out(hf[:, :, :S5_SLAB_ST]), _s5_state_out(hf[:, :, S5_SLAB_ST:])


def _softplus(x):
    return jnp.maximum(x, 0.0) + jnp.log1p(jnp.exp(-jnp.abs(x)))


def _lane_bcast(v, col, width):
    return jnp.broadcast_to(v[:, col:col + 1], (v.shape[0], width))


def _ssd_kernel(xbc_ref, z0_ref, z1_ref, dt_ref, sst_ref, dtb_ref, alog_ref, dsk_ref, nw_ref,
                cst_ref, cw_ref, cb_ref, y_ref, sso_ref, cso_ref,
                st_ref, yb_ref, xp_ref, xs_ref, *, clen, nk):
    k = pl.program_id(1)

    @pl.when(k == 0)
    def _():
        st_ref[...] = sst_ref[0]
        xp_ref[0:SUBLANES, :] = cst_ref[0]

    xp_ref[SUBLANES:SUBLANES + clen, :] = xbc_ref[...]
    xs_ref[...] = _causal_conv_silu(xp_ref, cw_ref, cb_ref, clen)
    xp_ref[0:SUBLANES, :] = xp_ref[clen:clen + SUBLANES, :]

    bmat = xs_ref[:, SSD_INNER:SSD_INNER + SSD_GROUPS * SSD_STATE].astype(BF16)
    cmat = xs_ref[:, SSD_INNER + SSD_GROUPS * SSD_STATE:].astype(BF16)

    dt = _softplus(dt_ref[...] + dtb_ref[...])
    da = dt * (-jnp.exp(alog_ref[...]))
    ri = lax.broadcasted_iota(jnp.int32, (clen, clen), 0)
    ci = lax.broadcasted_iota(jnp.int32, (clen, clen), 1)
    causal = ri >= ci
    acum = jnp.dot(causal.astype(F32), da, precision=HIGHEST, preferred_element_type=F32)
    eye = (lax.broadcasted_iota(jnp.int32, (LANES, LANES), 0)
           == lax.broadcasted_iota(jnp.int32, (LANES, LANES), 1)).astype(F32)
    acum_t = lax.dot_general(eye, acum, NT_DIMS, precision=HIGHEST, preferred_element_type=F32)
    dt_t = lax.dot_general(eye, dt, NT_DIMS, precision=HIGHEST, preferred_element_type=F32)
    eacum = jnp.exp(acum)
    tailw = jnp.exp(acum[clen - 1:clen, :] - acum) * dt
    sdec = jnp.exp(acum_t[:, clen - 1:clen])

    lo_lane = lax.broadcasted_iota(jnp.int32, (clen, LANES), 1) < SSD_HEAD_DIM
    lo_row = lax.broadcasted_iota(jnp.int32, (LANES, SSD_STATE), 0) < SSD_HEAD_DIM
    neg_inf = jnp.float32(-jnp.inf)

    for g in range(SSD_GROUPS):
        gs = slice(g * SSD_STATE, (g + 1) * SSD_STATE)
        cg = cmat[:, gs]
        bg = bmat[:, gs]
        cbm = lax.dot_general(cg, bg, NT_DIMS, preferred_element_type=F32)
        sg = st_ref[g * SSD_GROUP_CH:(g + 1) * SSD_GROUP_CH, :].astype(BF16)
        chs = lax.dot_general(cg, sg, NT_DIMS, preferred_element_type=F32)
        for j in range(SSD_GROUP_CH // LANES):
            h0 = (g * SSD_GROUP_CH + j * LANES) // SSD_HEAD_DIM
            h1 = h0 + 1
            cs = slice(g * SSD_GROUP_CH + j * LANES, g * SSD_GROUP_CH + (j + 1) * LANES)
            xf = xs_ref[:, cs]
            xb = xf.astype(BF16)
            ys = []
            for h in (h0, h1):
                diff = _lane_bcast(acum, h, clen) - acum_t[h:h + 1, :]
                w = cbm * jnp.exp(jnp.where(causal, diff, neg_inf)) * dt_t[h:h + 1, :]
                ys.append(jnp.dot(w.astype(BF16), xb, preferred_element_type=F32))
            y = jnp.where(lo_lane, ys[0], ys[1])
            ea = jnp.where(lo_lane, _lane_bcast(eacum, h0, LANES), _lane_bcast(eacum, h1, LANES))
            y = y + chs[:, j * LANES:(j + 1) * LANES] * ea + dsk_ref[:, cs] * xf
            yb_ref[:, cs] = y
            tw = jnp.where(lo_lane, _lane_bcast(tailw, h0, LANES), _lane_bcast(tailw, h1, LANES))
            txb = (xf * tw).astype(BF16)
            ds = lax.dot_general(txb, bg, (((0,), (0,)), ((), ())), preferred_element_type=F32)
            dec = jnp.where(lo_row,
                            jnp.broadcast_to(sdec[h0:h0 + 1, :], (LANES, SSD_STATE)),
                            jnp.broadcast_to(sdec[h1:h1 + 1, :], (LANES, SSD_STATE)))
            st_ref[cs, :] = st_ref[cs, :] * dec + ds

    half = SSD_INNER // 2
    for hf, z_ref in enumerate((z0_ref, z1_ref)):
        z = z_ref[...]
        yh = yb_ref[:, hf * half:(hf + 1) * half] * _silu(z)
        for gg in range(half // SSD_GROUP_CH):
            yg = yh[:, gg * SSD_GROUP_CH:(gg + 1) * SSD_GROUP_CH]
            ms = jnp.mean(yg * yg, axis=-1, keepdims=True)
            cs = slice(hf * half + gg * SSD_GROUP_CH, hf * half + (gg + 1) * SSD_GROUP_CH)
            y_ref[:, cs] = (yg * lax.rsqrt(ms + RMS_EPS) * nw_ref[:, cs]).astype(BF16)

    @pl.when(k == nk - 1)
    def _():
        sso_ref[0] = st_ref[...]
        cso_ref[0] = xp_ref[0:SUBLANES, :]


def _ssd_core(proj, sst, cst, conv_w, conv_b, dt_bias, a_log, d_exp, norm_w, bsz, tlen):
    clen = min(tlen, 128)
    nk = tlen // clen
    zblk = SSD_INNER // 2
    z_off = SSD_CONV_DIM // zblk
    dt_off = (SSD_CONV_DIM + SSD_INNER) // LANES
    row = lambda b, k: b * nk + k
    pad = lambda v: jnp.pad(v, (0, LANES - SSD_HEADS)).reshape(1, LANES)
    const = lambda shape: pl.BlockSpec(shape, lambda b, k: (0,) * len(shape))
    per_seq = lambda shape: pl.BlockSpec((1,) + shape, lambda b, k: (b, 0, 0))
    return pl.pallas_call(
        functools.partial(_ssd_kernel, clen=clen, nk=nk),
        grid=(bsz, nk),
        in_specs=[pl.BlockSpec((clen, SSD_CONV_DIM), lambda b, k: (row(b, k), 0)),
                  pl.BlockSpec((clen, zblk), lambda b, k: (row(b, k), z_off)),
                  pl.BlockSpec((clen, zblk), lambda b, k: (row(b, k), z_off + 1)),
                  pl.BlockSpec((clen, LANES), lambda b, k: (row(b, k), dt_off)),
                  per_seq((SSD_INNER, SSD_STATE)),
                  const((1, LANES)), const((1, LANES)), const((1, SSD_INNER)), const((1, SSD_INNER)),
                  per_seq((SUBLANES, SSD_CONV_DIM)), const((SSD_CONV, SSD_CONV_DIM)),
                  const((1, SSD_CONV_DIM))],
        out_specs=[pl.BlockSpec((clen, SSD_INNER), lambda b, k: (row(b, k), 0)),
                   per_seq((SSD_INNER, SSD_STATE)),
                   per_seq((SUBLANES, SSD_CONV_DIM))],
        out_shape=[jax.ShapeDtypeStruct((bsz * tlen, SSD_INNER), BF16),
                   jax.ShapeDtypeStruct((bsz, SSD_INNER, SSD_STATE), F32),
                   jax.ShapeDtypeStruct((bsz, SUBLANES, SSD_CONV_DIM), F32)],
        scratch_shapes=[pltpu.VMEM((SSD_INNER, SSD_STATE), F32),
                        pltpu.VMEM((clen, SSD_INNER), F32),
                        pltpu.VMEM((clen + SUBLANES, SSD_CONV_DIM), F32),
                        pltpu.VMEM((clen, SSD_CONV_DIM), F32)],
        compiler_params=_cparams(2),
        name="ssd_core",
    )(proj, proj, proj, proj, sst, pad(dt_bias), pad(a_log), d_exp.reshape(1, SSD_INNER),
      norm_w.reshape(1, SSD_INNER), cst, conv_w, conv_b.reshape(1, SSD_CONV_DIM))


def _ssd_in_weight(w_in):
    d = w_in.shape[0]
    z = w_in[:, :SSD_INNER]
    xbc = w_in[:, SSD_INNER:SSD_INNER + SSD_CONV_DIM]
    dt = w_in[:, SSD_INNER + SSD_CONV_DIM:]
    padw = SSD_PROJ_PAD - (SSD_CONV_DIM + SSD_INNER + SSD_HEADS)
    return jnp.concatenate([xbc, z, dt, jnp.zeros((d, padw), w_in.dtype)], axis=1).astype(BF16)


def _ssd_layer(x, bsz, tlen, conv_st, ssm_st, norm_g, w_in_p, conv_w, conv_b, dt_bias, a_log,
               d_skip, norm_w, w_out):
    hist = SSD_CONV - 1
    cst = jnp.pad(conv_st, ((0, 0), (SUBLANES - hist, 0), (0, 0)))
    sst = ssm_st.reshape(bsz, SSD_INNER, SSD_STATE)
    d_exp = jnp.repeat(d_skip, SSD_HEAD_DIM)
    proj = _norm_matmul(x, norm_g, w_in_p)
    y, sso, cso = _ssd_core(proj, sst, cst, conv_w, conv_b, dt_bias, a_log, d_exp, norm_w, bsz, tlen)
    x = _matmul_residual(x, y, w_out)
    return x, cso[:, SUBLANES - hist:], sso.reshape(bsz, SSD_HEADS, SSD_HEAD_DIM, SSD_STATE)


def _run(x3, s5_re, s5_im, conv_st, ssm_st, p):
    bsz, tlen, d = x3.shape
    x = x3.reshape(bsz * tlen, d)
    s5_re_out, s5_im_out, conv_out, ssm_out = [], [], [], []
    for i in range(DEPTH):
        j = i // 2
        if i % 2 == 0:
            x, hr, hi = _s5_layer(x, bsz, tlen, s5_re[j], s5_im[j], p["norm_mix"][i], p["s5"][j],
                                  p["s5_d"][j], p["s5_w_glu"][j])
            s5_re_out.append(hr)
            s5_im_out.append(hi)
        else:
            x, c, s = _ssd_layer(x, bsz, tlen, conv_st[j], ssm_st[j], p["norm_mix"][i], p["ssd_w_in"][j],
                                 p["ssd_conv_w"][j], p["ssd_conv_b"][j], p["ssd_dt_bias"][j],
                                 p["ssd_a_log"][j], p["ssd_d"][j], p["ssd_norm"][j], p["ssd_w_out"][j])
            conv_out.append(c)
            ssm_out.append(s)
        x = _mlp(x, p["norm_mlp"][i], p["mlp_w1"][i], p["mlp_w2"][i], p["norm_final"],
                 final_norm=(i == DEPTH - 1))
    return (x.reshape(bsz, tlen, d), jnp.stack(s5_re_out), jnp.stack(s5_im_out),
            jnp.stack(conv_out), jnp.stack(ssm_out))


def kernel(x_prompt, x_sample, state_s5_re, state_s5_im, state_ssd_conv, state_ssd_ssm, norm_mix, norm_mlp, norm_final, s5_lam_re, s5_lam_im, s5_log_dt, s5_b_re, s5_b_im, s5_c_re, s5_c_im, s5_d, s5_w_glu, ssd_w_in, ssd_conv_w, ssd_conv_b, ssd_dt_bias, ssd_a_log, ssd_d, ssd_norm, ssd_w_out, mlp_w1, mlp_w2):
    n_s5 = s5_lam_re.shape[0]
    n_ssd = ssd_w_in.shape[0]
    p = dict(
        norm_mix=norm_mix, norm_mlp=norm_mlp, norm_final=norm_final,
        s5=[_s5_params(s5_lam_re[j], s5_lam_im[j], s5_log_dt[j], s5_b_re[j], s5_b_im[j],
                       s5_c_re[j], s5_c_im[j]) for j in range(n_s5)],
        s5_d=s5_d, s5_w_glu=s5_w_glu.astype(BF16),
        ssd_w_in=[_ssd_in_weight(ssd_w_in[j]) for j in range(n_ssd)],
        ssd_conv_w=ssd_conv_w, ssd_conv_b=ssd_conv_b, ssd_dt_bias=ssd_dt_bias, ssd_a_log=ssd_a_log,
        ssd_d=ssd_d, ssd_norm=ssd_norm, ssd_w_out=ssd_w_out.astype(BF16),
        mlp_w1=mlp_w1.astype(BF16), mlp_w2=mlp_w2.astype(BF16),
    )
    bp = x_prompt.shape[0]
    zeros = lambda shape: jnp.zeros(shape, F32)
    y_p, s5_re_p, s5_im_p, conv_p, ssm_p = _run(
        x_prompt,
        zeros((n_s5, bp, S5_GROUPS, S5_STATE)), zeros((n_s5, bp, S5_GROUPS, S5_STATE)),
        zeros((n_ssd, bp, SSD_CONV - 1, SSD_CONV_DIM)),
        zeros((n_ssd, bp, SSD_HEADS, SSD_HEAD_DIM, SSD_STATE)), p)
    y_s, s5_re_s, s5_im_s, conv_s, ssm_s = _run(
        x_sample, state_s5_re, state_s5_im, state_ssd_conv, state_ssd_ssm, p)
    return (y_p, y_s, s5_re_p, s5_im_p, conv_p, ssm_p, s5_re_s, s5_im_s, conv_s, ssm_s)
```

```python
import functools

import jax
import jax.numpy as jnp
from jax import lax
from jax.experimental import pallas as pl
from jax.experimental.pallas import tpu as pltpu

F32 = jnp.float32
BF16 = jnp.bfloat16
HIGHEST = lax.Precision.HIGHEST

D_MODEL = 2048
DEPTH = 4
RMS_EPS = 1e-5
S5_GROUP = 16
S5_GROUPS = D_MODEL // S5_GROUP
S5_STATE = 64
S5_SLABS = 8
S5_SLAB_CH = D_MODEL // S5_SLABS
S5_SLAB_ST = (S5_GROUPS // S5_SLABS) * S5_STATE
SSD_INNER = 2 * D_MODEL
SSD_HEAD_DIM = 64
SSD_HEADS = SSD_INNER // SSD_HEAD_DIM
SSD_GROUPS = 8
SSD_STATE = 128
SSD_CONV = 4
SSD_CONV_DIM = SSD_INNER + 2 * SSD_GROUPS * SSD_STATE
SSD_GROUP_CH = SSD_INNER // SSD_GROUPS
LANES = 128
SUBLANES = 8
SSD_PROJ_PAD = 10752
D_FF = 4 * D_MODEL

VMEM_LIMIT = 56 * 1024 * 1024

NT_DIMS = (((1,), (1,)), ((), ()))


def _cparams(n_axes):
    return pltpu.CompilerParams(
        dimension_semantics=("arbitrary",) * n_axes, vmem_limit_bytes=VMEM_LIMIT)


def _rms(x, g):
    ms = jnp.mean(x * x, axis=-1, keepdims=True)
    return x * lax.rsqrt(ms + RMS_EPS) * g


def _silu(x):
    return 0.5 * x * (1.0 + jnp.tanh(0.5 * x))


def _row_tile(m, want):
    t = min(m, want)
    assert m % t == 0, (m, t)
    return t


def _resident(shape):
    return pl.BlockSpec(shape, lambda i: (0,) * len(shape), pipeline_mode=pl.Buffered(1))


def _norm_matmul_kernel(x_ref, g_ref, w_ref, o_ref):
    u = _rms(x_ref[...], g_ref[...]).astype(BF16)
    o_ref[...] = jnp.dot(u, w_ref[...], preferred_element_type=F32)


def _norm_matmul(x, g, w, tm=256):
    m, d = x.shape
    n = w.shape[1]
    tm = _row_tile(m, tm)
    return pl.pallas_call(
        _norm_matmul_kernel,
        grid=(m // tm,),
        in_specs=[pl.BlockSpec((tm, d), lambda i: (i, 0)),
                  _resident((1, d)),
                  _resident((d, n))],
        out_specs=pl.BlockSpec((tm, n), lambda i: (i, 0)),
        out_shape=jax.ShapeDtypeStruct((m, n), F32),
        compiler_params=_cparams(1),
        name="norm_matmul",
    )(x, g.reshape(1, d), w)


def _causal_conv_silu(stage_ref, cw_ref, cb_ref, rows):
    xc = cb_ref[...]
    for kk in range(SSD_CONV):
        r0 = SUBLANES - (SSD_CONV - 1) + kk
        xc = xc + cw_ref[kk:kk + 1, :] * stage_ref[r0:r0 + rows, :]
    return _silu(xc)


MLP_SPLITS = 2


def _mlp_part_kernel(*refs, first, final_norm):
    if first:
        x_ref, g_ref, w1_ref, w2_ref, gf_ref, o_ref = refs
        acc_ref = x_ref
    else:
        x_ref, acc_ref, g_ref, w1_ref, w2_ref, gf_ref, o_ref = refs
    u = _rms(x_ref[...], g_ref[...]).astype(BF16)
    a = jnp.dot(u, w1_ref[...], preferred_element_type=F32)
    a = jnp.maximum(a, 0.0)
    a = (a * a).astype(BF16)
    o = acc_ref[...] + jnp.dot(a, w2_ref[...], preferred_element_type=F32)
    if final_norm:
        o = _rms(o, gf_ref[...])
    o_ref[...] = o


def _mlp(x, g, w1, w2, gf, final_norm, tm=256):
    m, d = x.shape
    f = w1.shape[1]
    fs = f // MLP_SPLITS
    tm = _row_tile(m, tm)
    rows = pl.BlockSpec((tm, d), lambda i: (i, 0))
    acc = x
    for h in range(MLP_SPLITS):
        first = h == 0
        last = h == MLP_SPLITS - 1
        w_specs = [pl.BlockSpec((d, fs), lambda i, h=h: (0, h), pipeline_mode=pl.Buffered(1)),
                   pl.BlockSpec((fs, d), lambda i, h=h: (h, 0), pipeline_mode=pl.Buffered(1))]
        acc = pl.pallas_call(
            functools.partial(_mlp_part_kernel, first=first, final_norm=final_norm and last),
            grid=(m // tm,),
            in_specs=([rows] if first else [rows, rows]) + [_resident((1, d))] + w_specs
                     + [_resident((1, d))],
            out_specs=rows,
            out_shape=jax.ShapeDtypeStruct((m, d), F32),
            compiler_params=_cparams(1),
            name="mlp",
        )(*([x] if first else [x, acc]), g.reshape(1, d), w1, w2, gf.reshape(1, d))
    return acc


def _matmul_residual_kernel(x_ref, a_ref, w_ref, o_ref):
    o_ref[...] = x_ref[...] + jnp.dot(a_ref[...], w_ref[...], preferred_element_type=F32)


def _matmul_residual(x, a, w, tm=512):
    m, n = x.shape
    kdim = a.shape[1]
    tm = _row_tile(m, tm)
    return pl.pallas_call(
        _matmul_residual_kernel,
        grid=(m // tm,),
        in_specs=[pl.BlockSpec((tm, n), lambda i: (i, 0)),
                  pl.BlockSpec((tm, kdim), lambda i: (i, 0)),
                  _resident((kdim, n))],
        out_specs=pl.BlockSpec((tm, n), lambda i: (i, 0)),
        out_shape=jax.ShapeDtypeStruct((m, n), F32),
        compiler_params=_cparams(1),
        name="matmul_residual",
    )(x, a, w)


def _glu_residual_kernel(x_ref, a_ref, w_ref, o_ref):
    a = a_ref[...]
    n = o_ref.shape[1]
    ga = jnp.dot(a, w_ref[:, :n], preferred_element_type=F32)
    gb = jnp.dot(a, w_ref[:, n:], preferred_element_type=F32)
    o_ref[...] = x_ref[...] + ga * jax.nn.sigmoid(gb)


def _glu_residual(x, a, w, tm=512):
    m, n = x.shape
    kdim = a.shape[1]
    tm = _row_tile(m, tm)
    return pl.pallas_call(
        _glu_residual_kernel,
        grid=(m // tm,),
        in_specs=[pl.BlockSpec((tm, n), lambda i: (i, 0)),
                  pl.BlockSpec((tm, kdim), lambda i: (i, 0)),
                  _resident((kdim, 2 * n))],
        out_specs=pl.BlockSpec((tm, n), lambda i: (i, 0)),
        out_shape=jax.ShapeDtypeStruct((m, n), F32),
        compiler_params=_cparams(1),
        name="glu_residual",
    )(x, a, w)


S5_SCAN_CHUNKS = 2


S5_STEP_SLABS = 2


def _s5_kernel(x_ref, gn_ref, wb_ref, wc_ref, lam_ref, d_ref, h0_ref, g_ref, hf_ref,
               u_ref, hbuf_ref, hs_ref, *, bsz, tt, nk):
    k = pl.program_id(0)
    sp = pl.program_id(1)
    ns = S5_STEP_SLABS
    wch = ns * S5_SLAB_CH

    @pl.when((k == 0) & (sp == 0))
    def _():
        hs_ref[...] = h0_ref[...]

    @pl.when(sp == 0)
    def _():
        un = _rms(x_ref[...], gn_ref[...])
        for j in range(D_MODEL // wch):
            u_ref[j] = un[:, j * wch:(j + 1) * wch]

    u = u_ref[sp]
    for i in range(ns):
        ui = u[:, i * S5_SLAB_CH:(i + 1) * S5_SLAB_CH].astype(BF16)
        hbuf_ref[i] = jnp.dot(ui, wb_ref[sp * ns + i], preferred_element_type=F32)

    cw = S5_SLAB_ST // S5_SCAN_CHUNKS
    for i in range(ns):
        slab = sp * ns + i
        for q in range(S5_SCAN_CHUNKS):
            re = slice(q * cw, (q + 1) * cw)
            im = slice(S5_SLAB_ST + q * cw, S5_SLAB_ST + (q + 1) * cw)
            lr = lam_ref[slab, 0:1, re]
            li = lam_ref[slab, 1:2, re]

            def step(t, carry, i=i, re=re, im=im, lr=lr, li=li):
                hr, hi = carry
                rows = pl.ds(pl.multiple_of(t * bsz, bsz), bsz)
                nr = lr * hr - li * hi + hbuf_ref[i, rows, re]
                ni = lr * hi + li * hr + hbuf_ref[i, rows, im]
                hbuf_ref[i, rows, re] = nr
                hbuf_ref[i, rows, im] = ni
                return nr, ni

            hr, hi = lax.fori_loop(0, tt, step, (hs_ref[slab, :, re], hs_ref[slab, :, im]), unroll=True)
            hs_ref[slab, :, re] = hr
            hs_ref[slab, :, im] = hi

    d = d_ref[sp]
    for i in range(ns):
        cs = slice(i * S5_SLAB_CH, (i + 1) * S5_SLAB_CH)
        y = jnp.dot(hbuf_ref[i].astype(BF16), wc_ref[sp * ns + i], preferred_element_type=F32)
        y = y + d[:, cs] * u[:, cs]
        g_ref[:, cs] = jax.nn.gelu(y).astype(BF16)

    @pl.when((k == nk - 1) & (sp == S5_SLABS // ns - 1))
    def _():
        hf_ref[...] = hs_ref[...]


def _s5_core(xt, norm_g, wb, wc, lam, d, h0, bsz, tlen):
    tt = min(tlen, 512 // bsz)
    rows = tt * bsz
    nk = tlen // tt
    ns = S5_STEP_SLABS
    nsp = S5_SLABS // ns
    full = lambda shape: pl.BlockSpec(shape, lambda k, s: (0,) * len(shape),
                                      pipeline_mode=pl.Buffered(1))
    return pl.pallas_call(
        functools.partial(_s5_kernel, bsz=bsz, tt=tt, nk=nk),
        grid=(nk, nsp),
        in_specs=[pl.BlockSpec((rows, D_MODEL), lambda k, s: (k, 0)),
                  full((1, D_MODEL)),
                  full((S5_SLABS, S5_SLAB_CH, 2 * S5_SLAB_ST)),
                  full((S5_SLABS, 2 * S5_SLAB_ST, S5_SLAB_CH)),
                  full((S5_SLABS, 2, S5_SLAB_ST)),
                  full((nsp, 1, ns * S5_SLAB_CH)),
                  full((S5_SLABS, bsz, 2 * S5_SLAB_ST))],
        out_specs=[pl.BlockSpec((rows, ns * S5_SLAB_CH), lambda k, s: (k, s)),
                   pl.BlockSpec((S5_SLABS, bsz, 2 * S5_SLAB_ST), lambda k, s: (0, 0, 0))],
        out_shape=[jax.ShapeDtypeStruct((tlen * bsz, D_MODEL), BF16),
                   jax.ShapeDtypeStruct((S5_SLABS, bsz, 2 * S5_SLAB_ST), F32)],
        scratch_shapes=[pltpu.VMEM((nsp, rows, ns * S5_SLAB_CH), F32),
                        pltpu.VMEM((ns, rows, 2 * S5_SLAB_ST), F32),
                        pltpu.VMEM((S5_SLABS, bsz, 2 * S5_SLAB_ST), F32)],
        compiler_params=_cparams(2),
        name="s5_core",
    )(xt, norm_g.reshape(1, D_MODEL), wb, wc, lam, d.reshape(nsp, 1, ns * S5_SLAB_CH), h0)


def _s5_params(lam_re, lam_im, log_dt, b_re, b_im, c_re, c_im):
    dt = jnp.exp(log_dt)[:, None]
    er = jnp.exp(lam_re * dt)
    lbr = er * jnp.cos(lam_im * dt)
    lbi = er * jnp.sin(lam_im * dt)
    den = lam_re * lam_re + lam_im * lam_im
    qr = ((lbr - 1.0) * lam_re + lbi * lam_im) / den
    qi = (lbi * lam_re - (lbr - 1.0) * lam_im) / den
    bbr = qr[..., None] * b_re - qi[..., None] * b_im
    bbi = qr[..., None] * b_im + qi[..., None] * b_re
    gl = S5_GROUPS // S5_SLABS
    eye = jnp.eye(gl, dtype=F32)

    def pack_b(bb):
        bb = bb.reshape(S5_SLABS, gl, S5_STATE, S5_GROUP)
        w = jnp.einsum("sgpc,gh->sgchp", bb, eye)
        return w.reshape(S5_SLABS, S5_SLAB_CH, S5_SLAB_ST)

    def pack_c(cc):
        cc = cc.reshape(S5_SLABS, gl, S5_GROUP, S5_STATE)
        w = jnp.einsum("sgcp,gh->sgphc", cc, eye)
        return w.reshape(S5_SLABS, S5_SLAB_ST, S5_SLAB_CH)

    wb = jnp.concatenate([pack_b(bbr), pack_b(bbi)], axis=2).astype(BF16)
    wc = jnp.concatenate([pack_c(c_re), pack_c(-c_im)], axis=1).astype(BF16)
    lam = jnp.stack([lbr.reshape(S5_SLABS, S5_SLAB_ST), lbi.reshape(S5_SLABS, S5_SLAB_ST)], axis=1)
    return wb, wc, lam


def _s5_state_in(h):
    bsz = h.shape[0]
    return h.reshape(bsz, S5_SLABS, S5_SLAB_ST).transpose(1, 0, 2)


def _s5_state_out(h):
    bsz = h.shape[1]
    return h.transpose(1, 0, 2).reshape(bsz, S5_GROUPS, S5_STATE)


def _s5_layer(x, bsz, tlen, h_re, h_im, norm_g, prm, d_skip, w_glu):
    wb, wc, lam = prm
    xt = x.reshape(bsz, tlen, D_MODEL).transpose(1, 0, 2).reshape(tlen * bsz, D_MODEL)
    h0 = jnp.concatenate([_s5_state_in(h_re), _s5_state_in(h_im)], axis=2)
    gt, hf = _s5_core(xt, norm_g, wb, wc, lam, d_skip, h0, bsz, tlen)
    g = gt.reshape(tlen, bsz, D_MODEL).transpose(1, 0, 2).reshape(bsz * tlen, D_MODEL)
    x = _glu_residual(x, g, w_glu)
    return x, _s5_state_out(hf[:, :, :S5_SLAB_ST]), _s5_state_out(hf[:, :, S5_SLAB_ST:])


def _softplus(x):
    return jnp.maximum(x, 0.0) + jnp.log1p(jnp.exp(-jnp.abs(x)))


def _lane_bcast(v, col, width):
    return jnp.broadcast_to(v[:, col:col + 1], (v.shape[0], width))


def _ssd_kernel(xbc_ref, z0_ref, z1_ref, dt_ref, sst_ref, dtb_ref, alog_ref, dsk_ref, nw_ref,
                cst_ref, cw_ref, cb_ref, y_ref, sso_ref, cso_ref,
                st_ref, yb_ref, xp_ref, xs_ref, *, clen, nk):
    k = pl.program_id(1)

    @pl.when(k == 0)
    def _():
        st_ref[...] = sst_ref[0]
        xp_ref[0:SUBLANES, :] = cst_ref[0]

    xp_ref[SUBLANES:SUBLANES + clen, :] = xbc_ref[...]
    xs_ref[...] = _causal_conv_silu(xp_ref, cw_ref, cb_ref, clen)
    xp_ref[0:SUBLANES, :] = xp_ref[clen:clen + SUBLANES, :]

    bmat = xs_ref[:, SSD_INNER:SSD_INNER + SSD_GROUPS * SSD_STATE].astype(BF16)
    cmat = xs_ref[:, SSD_INNER + SSD_GROUPS * SSD_STATE:].astype(BF16)

    dt = _softplus(dt_ref[...] + dtb_ref[...])
    da = dt * (-jnp.exp(alog_ref[...]))
    ri = lax.broadcasted_iota(jnp.int32, (clen, clen), 0)
    ci = lax.broadcasted_iota(jnp.int32, (clen, clen), 1)
    causal = ri >= ci
    acum = jnp.dot(causal.astype(F32), da, precision=HIGHEST, preferred_element_type=F32)
    eye = (lax.broadcasted_iota(jnp.int32, (LANES, LANES), 0)
           == lax.broadcasted_iota(jnp.int32, (LANES, LANES), 1)).astype(F32)
    acum_t = lax.dot_general(eye, acum, NT_DIMS, precision=HIGHEST, preferred_element_type=F32)
    dt_t = lax.dot_general(eye, dt, NT_DIMS, precision=HIGHEST, preferred_element_type=F32)
    eacum = jnp.exp(acum)
    tailw = jnp.exp(acum[clen - 1:clen, :] - acum) * dt
    sdec = jnp.exp(acum_t[:, clen - 1:clen])

    lo_lane = lax.broadcasted_iota(jnp.int32, (clen, LANES), 1) < SSD_HEAD_DIM
    lo_row = lax.broadcasted_iota(jnp.int32, (LANES, SSD_STATE), 0) < SSD_HEAD_DIM
    neg_inf = jnp.float32(-jnp.inf)

    for g in range(SSD_GROUPS):
        gs = slice(g * SSD_STATE, (g + 1) * SSD_STATE)
        cg = cmat[:, gs]
        bg = bmat[:, gs]
        cbm = lax.dot_general(cg, bg, NT_DIMS, preferred_element_type=F32)
        sg = st_ref[g * SSD_GROUP_CH:(g + 1) * SSD_GROUP_CH, :].astype(BF16)
        chs = lax.dot_general(cg, sg, NT_DIMS, preferred_element_type=F32)
        for j in range(SSD_GROUP_CH // LANES):
            h0 = (g * SSD_GROUP_CH + j * LANES) // SSD_HEAD_DIM
            h1 = h0 + 1
            cs = slice(g * SSD_GROUP_CH + j * LANES, g * SSD_GROUP_CH + (j + 1) * LANES)
            xf = xs_ref[:, cs]
            xb = xf.astype(BF16)
            ys = []
            for h in (h0, h1):
                diff = _lane_bcast(acum, h, clen) - acum_t[h:h + 1, :]
                w = cbm * jnp.exp(jnp.where(causal, diff, neg_inf)) * dt_t[h:h + 1, :]
                ys.append(jnp.dot(w.astype(BF16), xb, preferred_element_type=F32))
            y = jnp.where(lo_lane, ys[0], ys[1])
            ea = jnp.where(lo_lane, _lane_bcast(eacum, h0, LANES), _lane_bcast(eacum, h1, LANES))
            y = y + chs[:, j * LANES:(j + 1) * LANES] * ea + dsk_ref[:, cs] * xf
            yb_ref[:, cs] = y
            tw = jnp.where(lo_lane, _lane_bcast(tailw, h0, LANES), _lane_bcast(tailw, h1, LANES))
            txb = (xf * tw).astype(BF16)
            ds = lax.dot_general(txb, bg, (((0,), (0,)), ((), ())), preferred_element_type=F32)
            dec = jnp.where(lo_row,
                            jnp.broadcast_to(sdec[h0:h0 + 1, :], (LANES, SSD_STATE)),
                            jnp.broadcast_to(sdec[h1:h1 + 1, :], (LANES, SSD_STATE)))
            st_ref[cs, :] = st_ref[cs, :] * dec + ds

    half = SSD_INNER // 2
    for hf, z_ref in enumerate((z0_ref, z1_ref)):
        z = z_ref[...]
        yh = yb_ref[:, hf * half:(hf + 1) * half] * _silu(z)
        for gg in range(half // SSD_GROUP_CH):
            yg = yh[:, gg * SSD_GROUP_CH:(gg + 1) * SSD_GROUP_CH]
            ms = jnp.mean(yg * yg, axis=-1, keepdims=True)
            cs = slice(hf * half + gg * SSD_GROUP_CH, hf * half + (gg + 1) * SSD_GROUP_CH)
            y_ref[:, cs] = (yg * lax.rsqrt(ms + RMS_EPS) * nw_ref[:, cs]).astype(BF16)

    @pl.when(k == nk - 1)
    def _():
        sso_ref[0] = st_ref[...]
        cso_ref[0] = xp_ref[0:SUBLANES, :]


def _ssd_core(xbc, zdt, sst, cst, conv_w, conv_b, dt_bias, a_log, d_exp, norm_w, bsz, tlen):
    clen = min(tlen, 128)
    nk = tlen // clen
    zblk = SSD_INNER // 2
    dt_off = SSD_INNER // LANES
    row = lambda b, k: b * nk + k
    pad = lambda v: jnp.pad(v, (0, LANES - SSD_HEADS)).reshape(1, LANES)
    const = lambda shape: pl.BlockSpec(shape, lambda b, k: (0,) * len(shape))
    per_seq = lambda shape: pl.BlockSpec((1,) + shape, lambda b, k: (b, 0, 0))
    return pl.pallas_call(
        functools.partial(_ssd_kernel, clen=clen, nk=nk),
        grid=(bsz, nk),
        in_specs=[pl.BlockSpec((clen, SSD_CONV_DIM), lambda b, k: (row(b, k), 0)),
                  pl.BlockSpec((clen, zblk), lambda b, k: (row(b, k), 0)),
                  pl.BlockSpec((clen, zblk), lambda b, k: (row(b, k), 1)),
                  pl.BlockSpec((clen, LANES), lambda b, k: (row(b, k), dt_off)),
                  per_seq((SSD_INNER, SSD_STATE)),
                  const((1, LANES)), const((1, LANES)), const((1, SSD_INNER)), const((1, SSD_INNER)),
                  per_seq((SUBLANES, SSD_CONV_DIM)), const((SSD_CONV, SSD_CONV_DIM)),
                  const((1, SSD_CONV_DIM))],
        out_specs=[pl.BlockSpec((clen, SSD_INNER), lambda b, k: (row(b, k), 0)),
                   per_seq((SSD_INNER, SSD_STATE)),
                   per_seq((SUBLANES, SSD_CONV_DIM))],
        out_shape=[jax.ShapeDtypeStruct((bsz * tlen, SSD_INNER), BF16),
                   jax.ShapeDtypeStruct((bsz, SSD_INNER, SSD_STATE), F32),
                   jax.ShapeDtypeStruct((bsz, SUBLANES, SSD_CONV_DIM), F32)],
        scratch_shapes=[pltpu.VMEM((SSD_INNER, SSD_STATE), F32),
                        pltpu.VMEM((clen, SSD_INNER), F32),
                        pltpu.VMEM((clen + SUBLANES, SSD_CONV_DIM), F32),
                        pltpu.VMEM((clen, SSD_CONV_DIM), F32)],
        compiler_params=_cparams(2),
        name="ssd_core",
    )(xbc, zdt, zdt, zdt, sst, pad(dt_bias), pad(a_log), d_exp.reshape(1, SSD_INNER),
      norm_w.reshape(1, SSD_INNER), cst, conv_w, conv_b.reshape(1, SSD_CONV_DIM))


def _ssd_in_weight(w_in):
    d = w_in.shape[0]
    z = w_in[:, :SSD_INNER]
    xbc = w_in[:, SSD_INNER:SSD_INNER + SSD_CONV_DIM]
    dt = w_in[:, SSD_INNER + SSD_CONV_DIM:]
    zdt = jnp.concatenate([z, dt, jnp.zeros((d, LANES - SSD_HEADS), w_in.dtype)], axis=1)
    return xbc.astype(BF16), zdt.astype(BF16)


def _ssd_layer(x, bsz, tlen, conv_st, ssm_st, norm_g, w_in_p, conv_w, conv_b, dt_bias, a_log,
               d_skip, norm_w, w_out):
    hist = SSD_CONV - 1
    cst = jnp.pad(conv_st, ((0, 0), (SUBLANES - hist, 0), (0, 0)))
    sst = ssm_st.reshape(bsz, SSD_INNER, SSD_STATE)
    d_exp = jnp.repeat(d_skip, SSD_HEAD_DIM)
    w_xbc, w_zdt = w_in_p
    xbc = _norm_matmul(x, norm_g, w_xbc)
    zdt = _norm_matmul(x, norm_g, w_zdt)
    y, sso, cso = _ssd_core(xbc, zdt, sst, cst, conv_w, conv_b, dt_bias, a_log, d_exp, norm_w,
                            bsz, tlen)
    x = _matmul_residual(x, y, w_out)
    return x, cso[:, SUBLANES - hist:], sso.reshape(bsz, SSD_HEADS, SSD_HEAD_DIM, SSD_STATE)


def _run(x3, s5_re, s5_im, conv_st, ssm_st, p):
    bsz, tlen, d = x3.shape
    x = x3.reshape(bsz * tlen, d)
    s5_re_out, s5_im_out, conv_out, ssm_out = [], [], [], []
    for i in range(DEPTH):
        j = i // 2
        if i % 2 == 0:
            x, hr, hi = _s5_layer(x, bsz, tlen, s5_re[j], s5_im[j], p["norm_mix"][i], p["s5"][j],
                                  p["s5_d"][j], p["s5_w_glu"][j])
            s5_re_out.append(hr)
            s5_im_out.append(hi)
        else:
            x, c, s = _ssd_layer(x, bsz, tlen, conv_st[j], ssm_st[j], p["norm_mix"][i], p["ssd_w_in"][j],
                                 p["ssd_conv_w"][j], p["ssd_conv_b"][j], p["ssd_dt_bias"][j],
                                 p["ssd_a_log"][j], p["ssd_d"][j], p["ssd_norm"][j], p["ssd_w_out"][j])
            conv_out.append(c)
            ssm_out.append(s)
        x = _mlp(x, p["norm_mlp"][i], p["mlp_w1"][i], p["mlp_w2"][i], p["norm_final"],
                 final_norm=(i == DEPTH - 1))
    return (x.reshape(bsz, tlen, d), jnp.stack(s5_re_out), jnp.stack(s5_im_out),
            jnp.stack(conv_out), jnp.stack(ssm_out))


def kernel(x_prompt, x_sample, state_s5_re, state_s5_im, state_ssd_conv, state_ssd_ssm, norm_mix, norm_mlp, norm_final, s5_lam_re, s5_lam_im, s5_log_dt, s5_b_re, s5_b_im, s5_c_re, s5_c_im, s5_d, s5_w_glu, ssd_w_in, ssd_conv_w, ssd_conv_b, ssd_dt_bias, ssd_a_log, ssd_d, ssd_norm, ssd_w_out, mlp_w1, mlp_w2):
    n_s5 = s5_lam_re.shape[0]
    n_ssd = ssd_w_in.shape[0]
    p = dict(
        norm_mix=norm_mix, norm_mlp=norm_mlp, norm_final=norm_final,
        s5=[_s5_params(s5_lam_re[j], s5_lam_im[j], s5_log_dt[j], s5_b_re[j], s5_b_im[j],
                       s5_c_re[j], s5_c_im[j]) for j in range(n_s5)],
        s5_d=s5_d, s5_w_glu=s5_w_glu.astype(BF16),
        ssd_w_in=[_ssd_in_weight(ssd_w_in[j]) for j in range(n_ssd)],
        ssd_conv_w=ssd_conv_w, ssd_conv_b=ssd_conv_b, ssd_dt_bias=ssd_dt_bias, ssd_a_log=ssd_a_log,
        ssd_d=ssd_d, ssd_norm=ssd_norm, ssd_w_out=ssd_w_out.astype(BF16),
        mlp_w1=mlp_w1.astype(BF16), mlp_w2=mlp_w2.astype(BF16),
    )
    bp = x_prompt.shape[0]
    zeros = lambda shape: jnp.zeros(shape, F32)
    y_p, s5_re_p, s5_im_p, conv_p, ssm_p = _run(
        x_prompt,
        zeros((n_s5, bp, S5_GROUPS, S5_STATE)), zeros((n_s5, bp, S5_GROUPS, S5_STATE)),
        zeros((n_ssd, bp, SSD_CONV - 1, SSD_CONV_DIM)),
        zeros((n_ssd, bp, SSD_HEADS, SSD_HEAD_DIM, SSD_STATE)), p)
    y_s, s5_re_s, s5_im_s, conv_s, ssm_s = _run(
        x_sample, state_s5_re, state_s5_im, state_ssd_conv, state_ssd_ssm, p)
    return (y_p, y_s, s5_re_p, s5_im_p, conv_p, ssm_p, s5_re_s, s5_im_s, conv_s, ssm_s)
```

```python
import functools

import jax
import jax.numpy as jnp
from jax import lax
from jax.experimental import pallas as pl
from jax.experimental.pallas import tpu as pltpu

F32 = jnp.float32
BF16 = jnp.bfloat16
HIGHEST = lax.Precision.HIGHEST

D_MODEL = 2048
DEPTH = 4
RMS_EPS = 1e-5
S5_GROUP = 16
S5_GROUPS = D_MODEL // S5_GROUP
S5_STATE = 64
S5_SLABS = 8
S5_SLAB_CH = D_MODEL // S5_SLABS
S5_SLAB_ST = (S5_GROUPS // S5_SLABS) * S5_STATE
SSD_INNER = 2 * D_MODEL
SSD_HEAD_DIM = 64
SSD_HEADS = SSD_INNER // SSD_HEAD_DIM
SSD_GROUPS = 8
SSD_STATE = 128
SSD_CONV = 4
SSD_CONV_DIM = SSD_INNER + 2 * SSD_GROUPS * SSD_STATE
SSD_GROUP_CH = SSD_INNER // SSD_GROUPS
LANES = 128
SUBLANES = 8
SSD_PROJ_PAD = 10752
D_FF = 4 * D_MODEL

VMEM_LIMIT = 56 * 1024 * 1024

NT_DIMS = (((1,), (1,)), ((), ()))


def _cparams(n_axes):
    return pltpu.CompilerParams(
        dimension_semantics=("arbitrary",) * n_axes, vmem_limit_bytes=VMEM_LIMIT)


def _rms(x, g):
    ms = jnp.mean(x * x, axis=-1, keepdims=True)
    return x * lax.rsqrt(ms + RMS_EPS) * g


def _silu(x):
    return 0.5 * x * (1.0 + jnp.tanh(0.5 * x))


def _row_tile(m, want):
    t = min(m, want)
    assert m % t == 0, (m, t)
    return t


def _resident(shape):
    return pl.BlockSpec(shape, lambda i: (0,) * len(shape), pipeline_mode=pl.Buffered(1))


def _norm_matmul_kernel(x_ref, g_ref, w_ref, o_ref):
    u = _rms(x_ref[...], g_ref[...]).astype(BF16)
    o_ref[...] = jnp.dot(u, w_ref[...], preferred_element_type=F32)


def _norm_matmul(x, g, w, tm=256):
    m, d = x.shape
    n = w.shape[1]
    tm = _row_tile(m, tm)
    return pl.pallas_call(
        _norm_matmul_kernel,
        grid=(m // tm,),
        in_specs=[pl.BlockSpec((tm, d), lambda i: (i, 0)),
                  _resident((1, d)),
                  _resident((d, n))],
        out_specs=pl.BlockSpec((tm, n), lambda i: (i, 0)),
        out_shape=jax.ShapeDtypeStruct((m, n), F32),
        compiler_params=_cparams(1),
        name="norm_matmul",
    )(x, g.reshape(1, d), w)


def _causal_conv_silu(stage_ref, cw_ref, cb_ref, rows):
    xc = cb_ref[...]
    for kk in range(SSD_CONV):
        r0 = SUBLANES - (SSD_CONV - 1) + kk
        xc = xc + cw_ref[kk:kk + 1, :] * stage_ref[r0:r0 + rows, :]
    return _silu(xc)


MLP_SPLITS = 2


def _mlp_part_kernel(*refs, first, final_norm):
    if first:
        x_ref, g_ref, w1_ref, w2_ref, gf_ref, o_ref = refs
        acc_ref = x_ref
    else:
        x_ref, acc_ref, g_ref, w1_ref, w2_ref, gf_ref, o_ref = refs
    u = _rms(x_ref[...], g_ref[...]).astype(BF16)
    a = jnp.dot(u, w1_ref[...], preferred_element_type=F32)
    a = jnp.maximum(a, 0.0)
    a = (a * a).astype(BF16)
    o = acc_ref[...] + jnp.dot(a, w2_ref[...], preferred_element_type=F32)
    if final_norm:
        o = _rms(o, gf_ref[...])
    o_ref[...] = o


def _mlp(x, g, w1, w2, gf, final_norm, tm=256):
    m, d = x.shape
    f = w1.shape[1]
    fs = f // MLP_SPLITS
    tm = _row_tile(m, tm)
    rows = pl.BlockSpec((tm, d), lambda i: (i, 0))
    acc = x
    for h in range(MLP_SPLITS):
        first = h == 0
        last = h == MLP_SPLITS - 1
        w_specs = [pl.BlockSpec((d, fs), lambda i, h=h: (0, h), pipeline_mode=pl.Buffered(1)),
                   pl.BlockSpec((fs, d), lambda i, h=h: (h, 0), pipeline_mode=pl.Buffered(1))]
        acc = pl.pallas_call(
            functools.partial(_mlp_part_kernel, first=first, final_norm=final_norm and last),
            grid=(m // tm,),
            in_specs=([rows] if first else [rows, rows]) + [_resident((1, d))] + w_specs
                     + [_resident((1, d))],
            out_specs=rows,
            out_shape=jax.ShapeDtypeStruct((m, d), F32),
            compiler_params=_cparams(1),
            name="mlp",
        )(*([x] if first else [x, acc]), g.reshape(1, d), w1, w2, gf.reshape(1, d))
    return acc


def _matmul_residual_kernel(x_ref, a_ref, w_ref, o_ref):
    o_ref[...] = x_ref[...] + jnp.dot(a_ref[...], w_ref[...], preferred_element_type=F32)


def _matmul_residual(x, a, w, tm=512):
    m, n = x.shape
    kdim = a.shape[1]
    tm = _row_tile(m, tm)
    return pl.pallas_call(
        _matmul_residual_kernel,
        grid=(m // tm,),
        in_specs=[pl.BlockSpec((tm, n), lambda i: (i, 0)),
                  pl.BlockSpec((tm, kdim), lambda i: (i, 0)),
                  _resident((kdim, n))],
        out_specs=pl.BlockSpec((tm, n), lambda i: (i, 0)),
        out_shape=jax.ShapeDtypeStruct((m, n), F32),
        compiler_params=_cparams(1),
        name="matmul_residual",
    )(x, a, w)


def _glu_residual_kernel(x_ref, a_ref, w_ref, o_ref):
    a = a_ref[...]
    n = o_ref.shape[1]
    ga = jnp.dot(a, w_ref[:, :n], preferred_element_type=F32)
    gb = jnp.dot(a, w_ref[:, n:], preferred_element_type=F32)
    o_ref[...] = x_ref[...] + ga * jax.nn.sigmoid(gb)


def _glu_residual(x, a, w, tm=512):
    m, n = x.shape
    kdim = a.shape[1]
    tm = _row_tile(m, tm)
    return pl.pallas_call(
        _glu_residual_kernel,
        grid=(m // tm,),
        in_specs=[pl.BlockSpec((tm, n), lambda i: (i, 0)),
                  pl.BlockSpec((tm, kdim), lambda i: (i, 0)),
                  _resident((kdim, 2 * n))],
        out_specs=pl.BlockSpec((tm, n), lambda i: (i, 0)),
        out_shape=jax.ShapeDtypeStruct((m, n), F32),
        compiler_params=_cparams(1),
        name="glu_residual",
    )(x, a, w)


S5_SCAN_CHUNKS = 2


S5_STEP_SLABS = 2


def _s5_kernel(x_ref, gn_ref, wb_ref, wc_ref, lam_ref, d_ref, h0_ref, g_ref, hf_ref,
               u_ref, hbuf_ref, hs_ref, *, bsz, tt, nk):
    k = pl.program_id(0)
    sp = pl.program_id(1)
    ns = S5_STEP_SLABS
    wch = ns * S5_SLAB_CH

    @pl.when((k == 0) & (sp == 0))
    def _():
        hs_ref[...] = h0_ref[...]

    @pl.when(sp == 0)
    def _():
        un = _rms(x_ref[...], gn_ref[...])
        for j in range(D_MODEL // wch):
            u_ref[j] = un[:, :, j * wch:(j + 1) * wch]

    u = jnp.swapaxes(u_ref[sp], 0, 1).reshape(tt * bsz, wch)
    for i in range(ns):
        ui = u[:, i * S5_SLAB_CH:(i + 1) * S5_SLAB_CH].astype(BF16)
        hbuf_ref[i] = jnp.dot(ui, wb_ref[sp * ns + i], preferred_element_type=F32)

    cw = S5_SLAB_ST // S5_SCAN_CHUNKS
    for i in range(ns):
        slab = sp * ns + i
        for q in range(S5_SCAN_CHUNKS):
            re = slice(q * cw, (q + 1) * cw)
            im = slice(S5_SLAB_ST + q * cw, S5_SLAB_ST + (q + 1) * cw)
            lr = lam_ref[slab, 0:1, re]
            li = lam_ref[slab, 1:2, re]

            def step(t, carry, i=i, re=re, im=im, lr=lr, li=li):
                hr, hi = carry
                rows = pl.ds(pl.multiple_of(t * bsz, bsz), bsz)
                nr = lr * hr - li * hi + hbuf_ref[i, rows, re]
                ni = lr * hi + li * hr + hbuf_ref[i, rows, im]
                hbuf_ref[i, rows, re] = nr
                hbuf_ref[i, rows, im] = ni
                return nr, ni

            hr, hi = lax.fori_loop(0, tt, step, (hs_ref[slab, :, re], hs_ref[slab, :, im]), unroll=True)
            hs_ref[slab, :, re] = hr
            hs_ref[slab, :, im] = hi

    d = d_ref[sp]
    for i in range(ns):
        cs = slice(i * S5_SLAB_CH, (i + 1) * S5_SLAB_CH)
        y = jnp.dot(hbuf_ref[i].astype(BF16), wc_ref[sp * ns + i], preferred_element_type=F32)
        y = jax.nn.gelu(y + d[:, cs] * u[:, cs])
        y = jnp.swapaxes(y.reshape(tt, bsz, S5_SLAB_CH), 0, 1)
        g_ref[:, :, cs] = y.astype(BF16)

    @pl.when((k == nk - 1) & (sp == S5_SLABS // ns - 1))
    def _():
        hf_ref[...] = hs_ref[...]


def _s5_core(x3, norm_g, wb, wc, lam, d, h0):
    bsz, tlen, _ = x3.shape
    tt = min(tlen, 512 // bsz)
    rows = tt * bsz
    nk = tlen // tt
    ns = S5_STEP_SLABS
    nsp = S5_SLABS // ns
    full = lambda shape: pl.BlockSpec(shape, lambda k, s: (0,) * len(shape),
                                      pipeline_mode=pl.Buffered(1))
    return pl.pallas_call(
        functools.partial(_s5_kernel, bsz=bsz, tt=tt, nk=nk),
        grid=(nk, nsp),
        in_specs=[pl.BlockSpec((bsz, tt, D_MODEL), lambda k, s: (0, k, 0)),
                  full((1, D_MODEL)),
                  full((S5_SLABS, S5_SLAB_CH, 2 * S5_SLAB_ST)),
                  full((S5_SLABS, 2 * S5_SLAB_ST, S5_SLAB_CH)),
                  full((S5_SLABS, 2, S5_SLAB_ST)),
                  full((nsp, 1, ns * S5_SLAB_CH)),
                  full((S5_SLABS, bsz, 2 * S5_SLAB_ST))],
        out_specs=[pl.BlockSpec((bsz, tt, ns * S5_SLAB_CH), lambda k, s: (0, k, s)),
                   pl.BlockSpec((S5_SLABS, bsz, 2 * S5_SLAB_ST), lambda k, s: (0, 0, 0))],
        out_shape=[jax.ShapeDtypeStruct((bsz, tlen, D_MODEL), BF16),
                   jax.ShapeDtypeStruct((S5_SLABS, bsz, 2 * S5_SLAB_ST), F32)],
        scratch_shapes=[pltpu.VMEM((nsp, bsz, tt, ns * S5_SLAB_CH), F32),
                        pltpu.VMEM((ns, rows, 2 * S5_SLAB_ST), F32),
                        pltpu.VMEM((S5_SLABS, bsz, 2 * S5_SLAB_ST), F32)],
        compiler_params=_cparams(2),
        name="s5_core",
    )(x3, norm_g.reshape(1, D_MODEL), wb, wc, lam, d.reshape(nsp, 1, ns * S5_SLAB_CH), h0)


def _s5_params(lam_re, lam_im, log_dt, b_re, b_im, c_re, c_im):
    dt = jnp.exp(log_dt)[:, None]
    er = jnp.exp(lam_re * dt)
    lbr = er * jnp.cos(lam_im * dt)
    lbi = er * jnp.sin(lam_im * dt)
    den = lam_re * lam_re + lam_im * lam_im
    qr = ((lbr - 1.0) * lam_re + lbi * lam_im) / den
    qi = (lbi * lam_re - (lbr - 1.0) * lam_im) / den
    bbr = qr[..., None] * b_re - qi[..., None] * b_im
    bbi = qr[..., None] * b_im + qi[..., None] * b_re
    gl = S5_GROUPS // S5_SLABS
    eye = jnp.eye(gl, dtype=F32)

    def pack_b(bb):
        bb = bb.reshape(S5_SLABS, gl, S5_STATE, S5_GROUP)
        w = jnp.einsum("sgpc,gh->sgchp", bb, eye)
        return w.reshape(S5_SLABS, S5_SLAB_CH, S5_SLAB_ST)

    def pack_c(cc):
        cc = cc.reshape(S5_SLABS, gl, S5_GROUP, S5_STATE)
        w = jnp.einsum("sgcp,gh->sgphc", cc, eye)
        return w.reshape(S5_SLABS, S5_SLAB_ST, S5_SLAB_CH)

    wb = jnp.concatenate([pack_b(bbr), pack_b(bbi)], axis=2).astype(BF16)
    wc = jnp.concatenate([pack_c(c_re), pack_c(-c_im)], axis=1).astype(BF16)
    lam = jnp.stack([lbr.reshape(S5_SLABS, S5_SLAB_ST), lbi.reshape(S5_SLABS, S5_SLAB_ST)], axis=1)
    return wb, wc, lam


def _s5_state_in(h):
    bsz = h.shape[0]
    return h.reshape(bsz, S5_SLABS, S5_SLAB_ST).transpose(1, 0, 2)


def _s5_state_out(h):
    bsz = h.shape[1]
    return h.transpose(1, 0, 2).reshape(bsz, S5_GROUPS, S5_STATE)


def _s5_layer(x, bsz, tlen, h_re, h_im, norm_g, prm, d_skip, w_glu):
    wb, wc, lam = prm
    h0 = jnp.concatenate([_s5_state_in(h_re), _s5_state_in(h_im)], axis=2)
    g, hf = _s5_core(x.reshape(bsz, tlen, D_MODEL), norm_g, wb, wc, lam, d_skip, h0)
    x = _glu_residual(x, g.reshape(bsz * tlen, D_MODEL), w_glu)
    return x, _s5_state_out(hf[:, :, :S5_SLAB_ST]), _s5_state_out(hf[:, :, S5_SLAB_ST:])


def _softplus(x):
    return jnp.maximum(x, 0.0) + jnp.log1p(jnp.exp(-jnp.abs(x)))


def _lane_bcast(v, col, width):
    return jnp.broadcast_to(v[:, col:col + 1], (v.shape[0], width))


def _ssd_kernel(xbc_ref, z0_ref, z1_ref, dt_ref, sst_ref, dtb_ref, alog_ref, dsk_ref, nw_ref,
                cst_ref, cw_ref, cb_ref, y_ref, sso_ref, cso_ref,
                st_ref, yb_ref, xp_ref, xs_ref, *, clen, nk):
    k = pl.program_id(1)

    @pl.when(k == 0)
    def _():
        st_ref[...] = sst_ref[0].reshape(SSD_INNER, SSD_STATE)
        xp_ref[0:SUBLANES, :] = cst_ref[0]

    xp_ref[SUBLANES:SUBLANES + clen, :] = xbc_ref[...]
    xs_ref[...] = _causal_conv_silu(xp_ref, cw_ref, cb_ref, clen)
    xp_ref[0:SUBLANES, :] = xp_ref[clen:clen + SUBLANES, :]

    bmat = xs_ref[:, SSD_INNER:SSD_INNER + SSD_GROUPS * SSD_STATE].astype(BF16)
    cmat = xs_ref[:, SSD_INNER + SSD_GROUPS * SSD_STATE:].astype(BF16)

    dt = _softplus(dt_ref[...] + dtb_ref[...])
    da = dt * (-jnp.exp(alog_ref[...]))
    ri = lax.broadcasted_iota(jnp.int32, (clen, clen), 0)
    ci = lax.broadcasted_iota(jnp.int32, (clen, clen), 1)
    causal = ri >= ci
    acum = jnp.dot(causal.astype(F32), da, precision=HIGHEST, preferred_element_type=F32)
    eye = (lax.broadcasted_iota(jnp.int32, (LANES, LANES), 0)
           == lax.broadcasted_iota(jnp.int32, (LANES, LANES), 1)).astype(F32)
    acum_t = lax.dot_general(eye, acum, NT_DIMS, precision=HIGHEST, preferred_element_type=F32)
    dt_t = lax.dot_general(eye, dt, NT_DIMS, precision=HIGHEST, preferred_element_type=F32)
    eacum = jnp.exp(acum)
    tailw = jnp.exp(acum[clen - 1:clen, :] - acum) * dt
    sdec = jnp.exp(acum_t[:, clen - 1:clen])

    lo_lane = lax.broadcasted_iota(jnp.int32, (clen, LANES), 1) < SSD_HEAD_DIM
    lo_row = lax.broadcasted_iota(jnp.int32, (LANES, SSD_STATE), 0) < SSD_HEAD_DIM
    neg_inf = jnp.float32(-jnp.inf)

    for g in range(SSD_GROUPS):
        gs = slice(g * SSD_STATE, (g + 1) * SSD_STATE)
        cg = cmat[:, gs]
        bg = bmat[:, gs]
        cbm = lax.dot_general(cg, bg, NT_DIMS, preferred_element_type=F32)
        sg = st_ref[g * SSD_GROUP_CH:(g + 1) * SSD_GROUP_CH, :].astype(BF16)
        chs = lax.dot_general(cg, sg, NT_DIMS, preferred_element_type=F32)
        for j in range(SSD_GROUP_CH // LANES):
            h0 = (g * SSD_GROUP_CH + j * LANES) // SSD_HEAD_DIM
            h1 = h0 + 1
            cs = slice(g * SSD_GROUP_CH + j * LANES, g * SSD_GROUP_CH + (j + 1) * LANES)
            xf = xs_ref[:, cs]
            xb = xf.astype(BF16)
            ys = []
            for h in (h0, h1):
                diff = _lane_bcast(acum, h, clen) - acum_t[h:h + 1, :]
                w = cbm * jnp.exp(jnp.where(causal, diff, neg_inf)) * dt_t[h:h + 1, :]
                ys.append(jnp.dot(w.astype(BF16), xb, preferred_element_type=F32))
            y = jnp.where(lo_lane, ys[0], ys[1])
            ea = jnp.where(lo_lane, _lane_bcast(eacum, h0, LANES), _lane_bcast(eacum, h1, LANES))
            y = y + chs[:, j * LANES:(j + 1) * LANES] * ea + dsk_ref[:, cs] * xf
            yb_ref[:, cs] = y
            tw = jnp.where(lo_lane, _lane_bcast(tailw, h0, LANES), _lane_bcast(tailw, h1, LANES))
            txb = (xf * tw).astype(BF16)
            ds = lax.dot_general(txb, bg, (((0,), (0,)), ((), ())), preferred_element_type=F32)
            dec = jnp.where(lo_row,
                            jnp.broadcast_to(sdec[h0:h0 + 1, :], (LANES, SSD_STATE)),
                            jnp.broadcast_to(sdec[h1:h1 + 1, :], (LANES, SSD_STATE)))
            st_ref[cs, :] = st_ref[cs, :] * dec + ds

    half = SSD_INNER // 2
    for hf, z_ref in enumerate((z0_ref, z1_ref)):
        z = z_ref[...]
        yh = yb_ref[:, hf * half:(hf + 1) * half] * _silu(z)
        for gg in range(half // SSD_GROUP_CH):
            yg = yh[:, gg * SSD_GROUP_CH:(gg + 1) * SSD_GROUP_CH]
            ms = jnp.mean(yg * yg, axis=-1, keepdims=True)
            cs = slice(hf * half + gg * SSD_GROUP_CH, hf * half + (gg + 1) * SSD_GROUP_CH)
            y_ref[:, cs] = (yg * lax.rsqrt(ms + RMS_EPS) * nw_ref[:, cs]).astype(BF16)

    @pl.when(k == nk - 1)
    def _():
        sso_ref[0] = st_ref[...].reshape(SSD_HEADS, SSD_HEAD_DIM, SSD_STATE)
        cso_ref[0] = xp_ref[0:SUBLANES, :]


def _ssd_core(xbc, zdt, sst, cst, conv_w, conv_b, dt_bias, a_log, d_exp, norm_w, bsz, tlen):
    clen = min(tlen, 128)
    nk = tlen // clen
    zblk = SSD_INNER // 2
    dt_off = SSD_INNER // LANES
    row = lambda b, k: b * nk + k
    pad = lambda v: jnp.pad(v, (0, LANES - SSD_HEADS)).reshape(1, LANES)
    const = lambda shape: pl.BlockSpec(shape, lambda b, k: (0,) * len(shape))
    per_seq = lambda shape: pl.BlockSpec((1,) + shape, lambda b, k: (b,) + (0,) * len(shape))
    state = (SSD_HEADS, SSD_HEAD_DIM, SSD_STATE)
    return pl.pallas_call(
        functools.partial(_ssd_kernel, clen=clen, nk=nk),
        grid=(bsz, nk),
        in_specs=[pl.BlockSpec((clen, SSD_CONV_DIM), lambda b, k: (row(b, k), 0)),
                  pl.BlockSpec((clen, zblk), lambda b, k: (row(b, k), 0)),
                  pl.BlockSpec((clen, zblk), lambda b, k: (row(b, k), 1)),
                  pl.BlockSpec((clen, LANES), lambda b, k: (row(b, k), dt_off)),
                  per_seq(state),
                  const((1, LANES)), const((1, LANES)), const((1, SSD_INNER)), const((1, SSD_INNER)),
                  per_seq((SUBLANES, SSD_CONV_DIM)), const((SSD_CONV, SSD_CONV_DIM)),
                  const((1, SSD_CONV_DIM))],
        out_specs=[pl.BlockSpec((clen, SSD_INNER), lambda b, k: (row(b, k), 0)),
                   per_seq(state),
                   per_seq((SUBLANES, SSD_CONV_DIM))],
        out_shape=[jax.ShapeDtypeStruct((bsz * tlen, SSD_INNER), BF16),
                   jax.ShapeDtypeStruct((bsz,) + state, F32),
                   jax.ShapeDtypeStruct((bsz, SUBLANES, SSD_CONV_DIM), F32)],
        scratch_shapes=[pltpu.VMEM((SSD_INNER, SSD_STATE), F32),
                        pltpu.VMEM((clen, SSD_INNER), F32),
                        pltpu.VMEM((clen + SUBLANES, SSD_CONV_DIM), F32),
                        pltpu.VMEM((clen, SSD_CONV_DIM), F32)],
        compiler_params=_cparams(2),
        name="ssd_core",
    )(xbc, zdt, zdt, zdt, sst, pad(dt_bias), pad(a_log), d_exp.reshape(1, SSD_INNER),
      norm_w.reshape(1, SSD_INNER), cst, conv_w, conv_b.reshape(1, SSD_CONV_DIM))


def _ssd_in_weight(w_in):
    d = w_in.shape[0]
    z = w_in[:, :SSD_INNER]
    xbc = w_in[:, SSD_INNER:SSD_INNER + SSD_CONV_DIM]
    dt = w_in[:, SSD_INNER + SSD_CONV_DIM:]
    zdt = jnp.concatenate([z, dt, jnp.zeros((d, LANES - SSD_HEADS), w_in.dtype)], axis=1)
    return xbc.astype(BF16), zdt.astype(BF16)


def _ssd_layer(x, bsz, tlen, conv_st, ssm_st, norm_g, w_in_p, conv_w, conv_b, dt_bias, a_log,
               d_skip, norm_w, w_out):
    hist = SSD_CONV - 1
    cst = jnp.pad(conv_st, ((0, 0), (SUBLANES - hist, 0), (0, 0)))
    d_exp = jnp.repeat(d_skip, SSD_HEAD_DIM)
    w_xbc, w_zdt = w_in_p
    xbc = _norm_matmul(x, norm_g, w_xbc)
    zdt = _norm_matmul(x, norm_g, w_zdt)
    y, sso, cso = _ssd_core(xbc, zdt, ssm_st, cst, conv_w, conv_b, dt_bias, a_log, d_exp, norm_w,
                            bsz, tlen)
    x = _matmul_residual(x, y, w_out)
    return x, cso[:, SUBLANES - hist:], sso


def _run(x3, s5_re, s5_im, conv_st, ssm_st, p):
    bsz, tlen, d = x3.shape
    x = x3.reshape(bsz * tlen, d)
    s5_re_out, s5_im_out, conv_out, ssm_out = [], [], [], []
    for i in range(DEPTH):
        j = i // 2
        if i % 2 == 0:
            x, hr, hi = _s5_layer(x, bsz, tlen, s5_re[j], s5_im[j], p["norm_mix"][i], p["s5"][j],
                                  p["s5_d"][j], p["s5_w_glu"][j])
            s5_re_out.append(hr)
            s5_im_out.append(hi)
        else:
            x, c, s = _ssd_layer(x, bsz, tlen, conv_st[j], ssm_st[j], p["norm_mix"][i], p["ssd_w_in"][j],
                                 p["ssd_conv_w"][j], p["ssd_conv_b"][j], p["ssd_dt_bias"][j],
                                 p["ssd_a_log"][j], p["ssd_d"][j], p["ssd_norm"][j], p["ssd_w_out"][j])
            conv_out.append(c)
            ssm_out.append(s)
        x = _mlp(x, p["norm_mlp"][i], p["mlp_w1"][i], p["mlp_w2"][i], p["norm_final"],
                 final_norm=(i == DEPTH - 1))
    return (x.reshape(bsz, tlen, d), jnp.stack(s5_re_out), jnp.stack(s5_im_out),
            jnp.stack(conv_out), jnp.stack(ssm_out))


def kernel(x_prompt, x_sample, state_s5_re, state_s5_im, state_ssd_conv, state_ssd_ssm, norm_mix, norm_mlp, norm_final, s5_lam_re, s5_lam_im, s5_log_dt, s5_b_re, s5_b_im, s5_c_re, s5_c_im, s5_d, s5_w_glu, ssd_w_in, ssd_conv_w, ssd_conv_b, ssd_dt_bias, ssd_a_log, ssd_d, ssd_norm, ssd_w_out, mlp_w1, mlp_w2):
    n_s5 = s5_lam_re.shape[0]
    n_ssd = ssd_w_in.shape[0]
    p = dict(
        norm_mix=norm_mix, norm_mlp=norm_mlp, norm_final=norm_final,
        s5=[_s5_params(s5_lam_re[j], s5_lam_im[j], s5_log_dt[j], s5_b_re[j], s5_b_im[j],
                       s5_c_re[j], s5_c_im[j]) for j in range(n_s5)],
        s5_d=s5_d, s5_w_glu=s5_w_glu.astype(BF16),
        ssd_w_in=[_ssd_in_weight(ssd_w_in[j]) for j in range(n_ssd)],
        ssd_conv_w=ssd_conv_w, ssd_conv_b=ssd_conv_b, ssd_dt_bias=ssd_dt_bias, ssd_a_log=ssd_a_log,
        ssd_d=ssd_d, ssd_norm=ssd_norm, ssd_w_out=ssd_w_out.astype(BF16),
        mlp_w1=mlp_w1.astype(BF16), mlp_w2=mlp_w2.astype(BF16),
    )
    bp = x_prompt.shape[0]
    zeros = lambda shape: jnp.zeros(shape, F32)
    y_p, s5_re_p, s5_im_p, conv_p, ssm_p = _run(
        x_prompt,
        zeros((n_s5, bp, S5_GROUPS, S5_STATE)), zeros((n_s5, bp, S5_GROUPS, S5_STATE)),
        zeros((n_ssd, bp, SSD_CONV - 1, SSD_CONV_DIM)),
        zeros((n_ssd, bp, SSD_HEADS, SSD_HEAD_DIM, SSD_STATE)), p)
    y_s, s5_re_s, s5_im_s, conv_s, ssm_s = _run(
        x_sample, state_s5_re, state_s5_im, state_ssd_conv, state_ssd_ssm, p)
    return (y_p, y_s, s5_re_p, s5_im_p, conv_p, ssm_p, s5_re_s, s5_im_s, conv_s, ssm_s)
```

```python
import functools

import jax
import jax.numpy as jnp
from jax import lax
from jax.experimental import pallas as pl
from jax.experimental.pallas import tpu as pltpu

F32 = jnp.float32
BF16 = jnp.bfloat16
HIGHEST = lax.Precision.HIGHEST

D_MODEL = 2048
DEPTH = 4
RMS_EPS = 1e-5
S5_GROUP = 16
S5_GROUPS = D_MODEL // S5_GROUP
S5_STATE = 64
S5_SLABS = 8
S5_SLAB_CH = D_MODEL // S5_SLABS
S5_SLAB_ST = (S5_GROUPS // S5_SLABS) * S5_STATE
SSD_INNER = 2 * D_MODEL
SSD_HEAD_DIM = 64
SSD_HEADS = SSD_INNER // SSD_HEAD_DIM
SSD_GROUPS = 8
SSD_STATE = 128
SSD_CONV = 4
SSD_CONV_DIM = SSD_INNER + 2 * SSD_GROUPS * SSD_STATE
SSD_GROUP_CH = SSD_INNER // SSD_GROUPS
LANES = 128
SUBLANES = 8
D_FF = 4 * D_MODEL

VMEM_LIMIT = 56 * 1024 * 1024

NT_DIMS = (((1,), (1,)), ((), ()))


def _cparams(n_axes):
    return pltpu.CompilerParams(
        dimension_semantics=("arbitrary",) * n_axes, vmem_limit_bytes=VMEM_LIMIT)


def _rms(x, g):
    ms = jnp.mean(x * x, axis=-1, keepdims=True)
    return x * lax.rsqrt(ms + RMS_EPS) * g


def _silu(x):
    return 0.5 * x * (1.0 + jnp.tanh(0.5 * x))


def _row_tile(m, want):
    t = min(m, want)
    assert m % t == 0, (m, t)
    return t


def _resident(shape):
    return pl.BlockSpec(shape, lambda i: (0,) * len(shape), pipeline_mode=pl.Buffered(1))


def _norm_matmul_kernel(x_ref, g_ref, *refs):
    *w_refs, o_ref = refs
    u = _rms(x_ref[...], g_ref[...]).astype(BF16)
    c0 = 0
    for w_ref in w_refs:
        c1 = c0 + w_ref.shape[1]
        o_ref[:, c0:c1] = jnp.dot(u, w_ref[...], preferred_element_type=F32)
        c0 = c1


def _norm_matmul(x, g, ws, tm=256):
    m, d = x.shape
    n = sum(w.shape[1] for w in ws)
    tm = _row_tile(m, tm)
    return pl.pallas_call(
        _norm_matmul_kernel,
        grid=(m // tm,),
        in_specs=[pl.BlockSpec((tm, d), lambda i: (i, 0)), _resident((1, d))]
                 + [_resident(w.shape) for w in ws],
        out_specs=pl.BlockSpec((tm, n), lambda i: (i, 0)),
        out_shape=jax.ShapeDtypeStruct((m, n), F32),
        compiler_params=_cparams(1),
        name="norm_matmul",
    )(x, g.reshape(1, d), *ws)


def _causal_conv_silu(stage_ref, cw_ref, cb_ref, rows):
    xc = cb_ref[...]
    for kk in range(SSD_CONV):
        r0 = SUBLANES - (SSD_CONV - 1) + kk
        xc = xc + cw_ref[kk:kk + 1, :] * stage_ref[r0:r0 + rows, :]
    return _silu(xc)


MLP_SPLITS = 2


def _mlp_part_kernel(*refs, first, final_norm):
    if first:
        x_ref, g_ref, w1_ref, w2_ref, gf_ref, o_ref = refs
        acc_ref = x_ref
    else:
        x_ref, acc_ref, g_ref, w1_ref, w2_ref, gf_ref, o_ref = refs
    u = _rms(x_ref[...], g_ref[...]).astype(BF16)
    a = jnp.dot(u, w1_ref[...], preferred_element_type=F32)
    a = jnp.maximum(a, 0.0)
    a = (a * a).astype(BF16)
    o = acc_ref[...] + jnp.dot(a, w2_ref[...], preferred_element_type=F32)
    if final_norm:
        o = _rms(o, gf_ref[...])
    o_ref[...] = o


def _mlp(x, g, w1, w2, layer, gf, final_norm, tm=256):
    m, d = x.shape
    f = w1.shape[2]
    fs = f // MLP_SPLITS
    tm = _row_tile(m, tm)
    rows = pl.BlockSpec((tm, d), lambda i: (i, 0))
    acc = x
    for h in range(MLP_SPLITS):
        first = h == 0
        last = h == MLP_SPLITS - 1
        w_specs = [pl.BlockSpec((None, d, fs), lambda i, h=h: (layer, 0, h),
                                pipeline_mode=pl.Buffered(1)),
                   pl.BlockSpec((None, fs, d), lambda i, h=h: (layer, h, 0),
                                pipeline_mode=pl.Buffered(1))]
        acc = pl.pallas_call(
            functools.partial(_mlp_part_kernel, first=first, final_norm=final_norm and last),
            grid=(m // tm,),
            in_specs=([rows] if first else [rows, rows]) + [_resident((1, d))] + w_specs
                     + [_resident((1, d))],
            out_specs=rows,
            out_shape=jax.ShapeDtypeStruct((m, d), F32),
            compiler_params=_cparams(1),
            name="mlp",
        )(*([x] if first else [x, acc]), g.reshape(1, d), w1, w2, gf.reshape(1, d))
    return acc


def _matmul_residual_kernel(x_ref, a_ref, w_ref, o_ref):
    o_ref[...] = x_ref[...] + jnp.dot(a_ref[...], w_ref[...], preferred_element_type=F32)


def _layer_resident(w, layer):
    return pl.BlockSpec((None,) + w.shape[1:], lambda i: (layer,) + (0,) * (w.ndim - 1),
                        pipeline_mode=pl.Buffered(1))


def _matmul_residual(x, a, w, layer, tm=512):
    m, n = x.shape
    kdim = a.shape[1]
    tm = _row_tile(m, tm)
    return pl.pallas_call(
        _matmul_residual_kernel,
        grid=(m // tm,),
        in_specs=[pl.BlockSpec((tm, n), lambda i: (i, 0)),
                  pl.BlockSpec((tm, kdim), lambda i: (i, 0)),
                  _layer_resident(w, layer)],
        out_specs=pl.BlockSpec((tm, n), lambda i: (i, 0)),
        out_shape=jax.ShapeDtypeStruct((m, n), F32),
        compiler_params=_cparams(1),
        name="matmul_residual",
    )(x, a, w)


def _glu_residual_kernel(x_ref, a_ref, w_ref, o_ref):
    a = a_ref[...]
    n = o_ref.shape[1]
    ga = jnp.dot(a, w_ref[:, :n], preferred_element_type=F32)
    gb = jnp.dot(a, w_ref[:, n:], preferred_element_type=F32)
    o_ref[...] = x_ref[...] + ga * jax.nn.sigmoid(gb)


def _glu_residual(x, a, w, layer, tm=512):
    m, n = x.shape
    kdim = a.shape[1]
    tm = _row_tile(m, tm)
    return pl.pallas_call(
        _glu_residual_kernel,
        grid=(m // tm,),
        in_specs=[pl.BlockSpec((tm, n), lambda i: (i, 0)),
                  pl.BlockSpec((tm, kdim), lambda i: (i, 0)),
                  _layer_resident(w, layer)],
        out_specs=pl.BlockSpec((tm, n), lambda i: (i, 0)),
        out_shape=jax.ShapeDtypeStruct((m, n), F32),
        compiler_params=_cparams(1),
        name="glu_residual",
    )(x, a, w)


S5_SCAN_CHUNKS = 2


S5_STEP_SLABS = 2


def _s5_kernel(x_ref, gn_ref, wb_ref, wc_ref, lam_ref, d_ref, h0_ref, g_ref, hf_ref,
               u_ref, hbuf_ref, hs_ref, *, bsz, tt, nk):
    k = pl.program_id(0)
    sp = pl.program_id(1)
    ns = S5_STEP_SLABS
    wch = ns * S5_SLAB_CH

    @pl.when((k == 0) & (sp == 0))
    def _():
        hs_ref[...] = h0_ref[...]

    @pl.when(sp == 0)
    def _():
        un = _rms(x_ref[...], gn_ref[...])
        for j in range(D_MODEL // wch):
            u_ref[j] = un[:, :, j * wch:(j + 1) * wch]

    u = jnp.swapaxes(u_ref[sp], 0, 1).reshape(tt * bsz, wch)
    for i in range(ns):
        ui = u[:, i * S5_SLAB_CH:(i + 1) * S5_SLAB_CH].astype(BF16)
        hbuf_ref[i] = jnp.dot(ui, wb_ref[sp * ns + i], preferred_element_type=F32)

    cw = S5_SLAB_ST // S5_SCAN_CHUNKS
    for i in range(ns):
        slab = sp * ns + i
        for q in range(S5_SCAN_CHUNKS):
            re = slice(q * cw, (q + 1) * cw)
            im = slice(S5_SLAB_ST + q * cw, S5_SLAB_ST + (q + 1) * cw)
            lr = lam_ref[slab, 0:1, re]
            li = lam_ref[slab, 1:2, re]

            def step(t, carry, i=i, re=re, im=im, lr=lr, li=li):
                hr, hi = carry
                rows = pl.ds(pl.multiple_of(t * bsz, bsz), bsz)
                nr = lr * hr - li * hi + hbuf_ref[i, rows, re]
                ni = lr * hi + li * hr + hbuf_ref[i, rows, im]
                hbuf_ref[i, rows, re] = nr
                hbuf_ref[i, rows, im] = ni
                return nr, ni

            hr, hi = lax.fori_loop(0, tt, step, (hs_ref[slab, :, re], hs_ref[slab, :, im]), unroll=True)
            hs_ref[slab, :, re] = hr
            hs_ref[slab, :, im] = hi

    d = d_ref[sp]
    for i in range(ns):
        cs = slice(i * S5_SLAB_CH, (i + 1) * S5_SLAB_CH)
        y = jnp.dot(hbuf_ref[i].astype(BF16), wc_ref[sp * ns + i], preferred_element_type=F32)
        y = jax.nn.gelu(y + d[:, cs] * u[:, cs])
        y = jnp.swapaxes(y.reshape(tt, bsz, S5_SLAB_CH), 0, 1)
        g_ref[:, :, cs] = y.astype(BF16)

    @pl.when((k == nk - 1) & (sp == S5_SLABS // ns - 1))
    def _():
        hf_ref[...] = hs_ref[...]


def _s5_core(x3, norm_g, wb, wc, lam, d, h0):
    bsz, tlen, _ = x3.shape
    tt = min(tlen, 512 // bsz)
    rows = tt * bsz
    nk = tlen // tt
    ns = S5_STEP_SLABS
    nsp = S5_SLABS // ns
    full = lambda shape: pl.BlockSpec(shape, lambda k, s: (0,) * len(shape),
                                      pipeline_mode=pl.Buffered(1))
    return pl.pallas_call(
        functools.partial(_s5_kernel, bsz=bsz, tt=tt, nk=nk),
        grid=(nk, nsp),
        in_specs=[pl.BlockSpec((bsz, tt, D_MODEL), lambda k, s: (0, k, 0)),
                  full((1, D_MODEL)),
                  full((S5_SLABS, S5_SLAB_CH, 2 * S5_SLAB_ST)),
                  full((S5_SLABS, 2 * S5_SLAB_ST, S5_SLAB_CH)),
                  full((S5_SLABS, 2, S5_SLAB_ST)),
                  full((nsp, 1, ns * S5_SLAB_CH)),
                  full((S5_SLABS, bsz, 2 * S5_SLAB_ST))],
        out_specs=[pl.BlockSpec((bsz, tt, ns * S5_SLAB_CH), lambda k, s: (0, k, s)),
                   pl.BlockSpec((S5_SLABS, bsz, 2 * S5_SLAB_ST), lambda k, s: (0, 0, 0))],
        out_shape=[jax.ShapeDtypeStruct((bsz, tlen, D_MODEL), BF16),
                   jax.ShapeDtypeStruct((S5_SLABS, bsz, 2 * S5_SLAB_ST), F32)],
        scratch_shapes=[pltpu.VMEM((nsp, bsz, tt, ns * S5_SLAB_CH), F32),
                        pltpu.VMEM((ns, rows, 2 * S5_SLAB_ST), F32),
                        pltpu.VMEM((S5_SLABS, bsz, 2 * S5_SLAB_ST), F32)],
        compiler_params=_cparams(2),
        name="s5_core",
    )(x3, norm_g.reshape(1, D_MODEL), wb, wc, lam, d.reshape(nsp, 1, ns * S5_SLAB_CH), h0)


def _s5_params(lam_re, lam_im, log_dt, b_re, b_im, c_re, c_im):
    dt = jnp.exp(log_dt)[:, None]
    er = jnp.exp(lam_re * dt)
    lbr = er * jnp.cos(lam_im * dt)
    lbi = er * jnp.sin(lam_im * dt)
    den = lam_re * lam_re + lam_im * lam_im
    qr = ((lbr - 1.0) * lam_re + lbi * lam_im) / den
    qi = (lbi * lam_re - (lbr - 1.0) * lam_im) / den
    bbr = qr[..., None] * b_re - qi[..., None] * b_im
    bbi = qr[..., None] * b_im + qi[..., None] * b_re
    gl = S5_GROUPS // S5_SLABS
    eye = jnp.eye(gl, dtype=F32)

    def pack_b(bb):
        bb = bb.reshape(S5_SLABS, gl, S5_STATE, S5_GROUP)
        w = jnp.einsum("sgpc,gh->sgchp", bb, eye)
        return w.reshape(S5_SLABS, S5_SLAB_CH, S5_SLAB_ST)

    def pack_c(cc):
        cc = cc.reshape(S5_SLABS, gl, S5_GROUP, S5_STATE)
        w = jnp.einsum("sgcp,gh->sgphc", cc, eye)
        return w.reshape(S5_SLABS, S5_SLAB_ST, S5_SLAB_CH)

    wb = jnp.concatenate([pack_b(bbr), pack_b(bbi)], axis=2).astype(BF16)
    wc = jnp.concatenate([pack_c(c_re), pack_c(-c_im)], axis=1).astype(BF16)
    lam = jnp.stack([lbr.reshape(S5_SLABS, S5_SLAB_ST), lbi.reshape(S5_SLABS, S5_SLAB_ST)], axis=1)
    return wb, wc, lam


def _s5_state_in(h):
    bsz = h.shape[0]
    return h.reshape(bsz, S5_SLABS, S5_SLAB_ST).transpose(1, 0, 2)


def _s5_state_out(h):
    bsz = h.shape[1]
    return h.transpose(1, 0, 2).reshape(bsz, S5_GROUPS, S5_STATE)


def _s5_layer(x, bsz, tlen, h_re, h_im, norm_g, prm, d_skip, w_glu, layer):
    wb, wc, lam = prm
    h0 = jnp.concatenate([_s5_state_in(h_re), _s5_state_in(h_im)], axis=2)
    g, hf = _s5_core(x.reshape(bsz, tlen, D_MODEL), norm_g, wb, wc, lam, d_skip, h0)
    x = _glu_residual(x, g.reshape(bsz * tlen, D_MODEL), w_glu, layer)
    return x, _s5_state_out(hf[:, :, :S5_SLAB_ST]), _s5_state_out(hf[:, :, S5_SLAB_ST:])


def _softplus(x):
    return jnp.maximum(x, 0.0) + jnp.log1p(jnp.exp(-jnp.abs(x)))


def _lane_bcast(v, col, width):
    return jnp.broadcast_to(v[:, col:col + 1], (v.shape[0], width))


def _ssd_kernel(xbc_ref, z0_ref, z1_ref, dt_ref, sst_ref, dtb_ref, alog_ref, dsk_ref, nw_ref,
                cst_ref, cw_ref, cb_ref, y_ref, sso_ref, cso_ref,
                st_ref, yb_ref, xp_ref, xs_ref, *, clen, nk):
    k = pl.program_id(1)

    @pl.when(k == 0)
    def _():
        st_ref[...] = sst_ref[...].reshape(SSD_INNER, SSD_STATE)
        xp_ref[0:SUBLANES, :] = cst_ref[0]

    xp_ref[SUBLANES:SUBLANES + clen, :] = xbc_ref[...]
    xs_ref[...] = _causal_conv_silu(xp_ref, cw_ref, cb_ref, clen)
    xp_ref[0:SUBLANES, :] = xp_ref[clen:clen + SUBLANES, :]

    bmat = xs_ref[:, SSD_INNER:SSD_INNER + SSD_GROUPS * SSD_STATE].astype(BF16)
    cmat = xs_ref[:, SSD_INNER + SSD_GROUPS * SSD_STATE:].astype(BF16)

    dt = _softplus(dt_ref[...] + dtb_ref[...])
    da = dt * (-jnp.exp(alog_ref[...]))
    ri = lax.broadcasted_iota(jnp.int32, (clen, clen), 0)
    ci = lax.broadcasted_iota(jnp.int32, (clen, clen), 1)
    causal = ri >= ci
    acum = jnp.dot(causal.astype(F32), da, precision=HIGHEST, preferred_element_type=F32)
    eye = (lax.broadcasted_iota(jnp.int32, (LANES, LANES), 0)
           == lax.broadcasted_iota(jnp.int32, (LANES, LANES), 1)).astype(F32)
    acum_t = lax.dot_general(eye, acum, NT_DIMS, precision=HIGHEST, preferred_element_type=F32)
    dt_t = lax.dot_general(eye, dt, NT_DIMS, precision=HIGHEST, preferred_element_type=F32)
    eacum = jnp.exp(acum)
    tailw = jnp.exp(acum[clen - 1:clen, :] - acum) * dt
    sdec = jnp.exp(acum_t[:, clen - 1:clen])

    lo_lane = lax.broadcasted_iota(jnp.int32, (clen, LANES), 1) < SSD_HEAD_DIM
    lo_row = lax.broadcasted_iota(jnp.int32, (LANES, SSD_STATE), 0) < SSD_HEAD_DIM
    neg_inf = jnp.float32(-jnp.inf)

    for g in range(SSD_GROUPS):
        gs = slice(g * SSD_STATE, (g + 1) * SSD_STATE)
        cg = cmat[:, gs]
        bg = bmat[:, gs]
        cbm = lax.dot_general(cg, bg, NT_DIMS, preferred_element_type=F32)
        sg = st_ref[g * SSD_GROUP_CH:(g + 1) * SSD_GROUP_CH, :].astype(BF16)
        chs = lax.dot_general(cg, sg, NT_DIMS, preferred_element_type=F32)
        for j in range(SSD_GROUP_CH // LANES):
            h0 = (g * SSD_GROUP_CH + j * LANES) // SSD_HEAD_DIM
            h1 = h0 + 1
            cs = slice(g * SSD_GROUP_CH + j * LANES, g * SSD_GROUP_CH + (j + 1) * LANES)
            xf = xs_ref[:, cs]
            xb = xf.astype(BF16)
            ys = []
            for h in (h0, h1):
                diff = _lane_bcast(acum, h, clen) - acum_t[h:h + 1, :]
                w = cbm * jnp.exp(jnp.where(causal, diff, neg_inf)) * dt_t[h:h + 1, :]
                ys.append(jnp.dot(w.astype(BF16), xb, preferred_element_type=F32))
            y = jnp.where(lo_lane, ys[0], ys[1])
            ea = jnp.where(lo_lane, _lane_bcast(eacum, h0, LANES), _lane_bcast(eacum, h1, LANES))
            y = y + chs[:, j * LANES:(j + 1) * LANES] * ea + dsk_ref[:, cs] * xf
            yb_ref[:, cs] = y
            tw = jnp.where(lo_lane, _lane_bcast(tailw, h0, LANES), _lane_bcast(tailw, h1, LANES))
            txb = (xf * tw).astype(BF16)
            ds = lax.dot_general(txb, bg, (((0,), (0,)), ((), ())), preferred_element_type=F32)
            dec = jnp.where(lo_row,
                            jnp.broadcast_to(sdec[h0:h0 + 1, :], (LANES, SSD_STATE)),
                            jnp.broadcast_to(sdec[h1:h1 + 1, :], (LANES, SSD_STATE)))
            st_ref[cs, :] = st_ref[cs, :] * dec + ds

    half = SSD_INNER // 2
    for hf, z_ref in enumerate((z0_ref, z1_ref)):
        z = z_ref[...]
        yh = yb_ref[:, hf * half:(hf + 1) * half] * _silu(z)
        for gg in range(half // SSD_GROUP_CH):
            yg = yh[:, gg * SSD_GROUP_CH:(gg + 1) * SSD_GROUP_CH]
            ms = jnp.mean(yg * yg, axis=-1, keepdims=True)
            cs = slice(hf * half + gg * SSD_GROUP_CH, hf * half + (gg + 1) * SSD_GROUP_CH)
            y_ref[:, cs] = (yg * lax.rsqrt(ms + RMS_EPS) * nw_ref[:, cs]).astype(BF16)

    @pl.when(k == nk - 1)
    def _():
        sso_ref[...] = st_ref[...].reshape(SSD_HEADS, SSD_HEAD_DIM, SSD_STATE)
        cso_ref[0] = xp_ref[0:SUBLANES, :]


def _ssd_kernel_with_acc(*refs, **kw):
    _ssd_kernel(*refs[:12], *refs[13:], **kw)


def _ssd_core(xbc, zdt, sst_all, sso_all, layer, cst, conv_w, conv_b, dt_bias, a_log, d_exp, norm_w,
              bsz, tlen):
    clen = min(tlen, 128)
    nk = tlen // clen
    zblk = SSD_INNER // 2
    dt_off = SSD_INNER // LANES
    row = lambda b, k: b * nk + k
    pad = lambda v: jnp.pad(v, (0, LANES - SSD_HEADS)).reshape(1, LANES)
    const = lambda shape: pl.BlockSpec(shape, lambda b, k: (0,) * len(shape))
    per_seq = lambda shape: pl.BlockSpec((1,) + shape, lambda b, k: (b,) + (0,) * len(shape))
    state = (SSD_HEADS, SSD_HEAD_DIM, SSD_STATE)
    layer_state = pl.BlockSpec((None, None) + state, lambda b, k: (layer, b, 0, 0, 0))
    in_specs = [pl.BlockSpec((clen, SSD_CONV_DIM), lambda b, k: (row(b, k), 0)),
                pl.BlockSpec((clen, zblk), lambda b, k: (row(b, k), 0)),
                pl.BlockSpec((clen, zblk), lambda b, k: (row(b, k), 1)),
                pl.BlockSpec((clen, LANES), lambda b, k: (row(b, k), dt_off)),
                layer_state,
                const((1, LANES)), const((1, LANES)), const((1, SSD_INNER)), const((1, SSD_INNER)),
                per_seq((SUBLANES, SSD_CONV_DIM)), const((SSD_CONV, SSD_CONV_DIM)),
                const((1, SSD_CONV_DIM))]
    args = [xbc, zdt, zdt, zdt, sst_all, pad(dt_bias), pad(a_log), d_exp.reshape(1, SSD_INNER),
            norm_w.reshape(1, SSD_INNER), cst, conv_w, conv_b.reshape(1, SSD_CONV_DIM)]
    body, aliases = _ssd_kernel, {}
    if sso_all is not None:
        body, aliases = _ssd_kernel_with_acc, {len(args): 1}
        in_specs.append(pl.BlockSpec(memory_space=pl.ANY))
        args.append(sso_all)
    return pl.pallas_call(
        functools.partial(body, clen=clen, nk=nk),
        grid=(bsz, nk),
        in_specs=in_specs,
        out_specs=[pl.BlockSpec((clen, SSD_INNER), lambda b, k: (row(b, k), 0)),
                   layer_state,
                   per_seq((SUBLANES, SSD_CONV_DIM))],
        out_shape=[jax.ShapeDtypeStruct((bsz * tlen, SSD_INNER), BF16),
                   jax.ShapeDtypeStruct(sst_all.shape, F32),
                   jax.ShapeDtypeStruct((bsz, SUBLANES, SSD_CONV_DIM), F32)],
        scratch_shapes=[pltpu.VMEM((SSD_INNER, SSD_STATE), F32),
                        pltpu.VMEM((clen, SSD_INNER), F32),
                        pltpu.VMEM((clen + SUBLANES, SSD_CONV_DIM), F32),
                        pltpu.VMEM((clen, SSD_CONV_DIM), F32)],
        input_output_aliases=aliases,
        compiler_params=_cparams(2),
        name="ssd_core",
    )(*args)


def _ssd_in_weight(w_in):
    z = w_in[:, :SSD_INNER].astype(BF16)
    xbc = w_in[:, SSD_INNER:SSD_INNER + SSD_CONV_DIM].astype(BF16)
    dt = jnp.pad(w_in[:, SSD_INNER + SSD_CONV_DIM:].astype(BF16), ((0, 0), (0, LANES - SSD_HEADS)))
    return xbc, z, dt


def _ssd_layer(x, bsz, tlen, conv_st, ssm_all, sso_all, layer, norm_g, w_in_p, conv_w, conv_b,
               dt_bias, a_log, d_skip, norm_w, w_out):
    hist = SSD_CONV - 1
    cst = jnp.pad(conv_st, ((0, 0), (SUBLANES - hist, 0), (0, 0)))
    d_exp = jnp.repeat(d_skip, SSD_HEAD_DIM)
    w_xbc, w_z, w_dt = w_in_p
    xbc = _norm_matmul(x, norm_g, [w_xbc])
    zdt = _norm_matmul(x, norm_g, [w_z, w_dt])
    y, sso_all, cso = _ssd_core(xbc, zdt, ssm_all, sso_all, layer, cst, conv_w, conv_b, dt_bias,
                                a_log, d_exp, norm_w, bsz, tlen)
    x = _matmul_residual(x, y, w_out, layer)
    return x, cso[:, SUBLANES - hist:], sso_all


def _run(x3, s5_re, s5_im, conv_st, ssm_st, p):
    bsz, tlen, d = x3.shape
    x = x3.reshape(bsz * tlen, d)
    s5_re_out, s5_im_out, conv_out = [], [], []
    ssm_out = None
    for i in range(DEPTH):
        j = i // 2
        if i % 2 == 0:
            x, hr, hi = _s5_layer(x, bsz, tlen, s5_re[j], s5_im[j], p["norm_mix"][i], p["s5"][j],
                                  p["s5_d"][j], p["s5_w_glu"], j)
            s5_re_out.append(hr)
            s5_im_out.append(hi)
        else:
            x, c, ssm_out = _ssd_layer(
                x, bsz, tlen, conv_st[j], ssm_st, ssm_out, j, p["norm_mix"][i], p["ssd_w_in"][j],
                p["ssd_conv_w"][j], p["ssd_conv_b"][j], p["ssd_dt_bias"][j], p["ssd_a_log"][j],
                p["ssd_d"][j], p["ssd_norm"][j], p["ssd_w_out"])
            conv_out.append(c)
        x = _mlp(x, p["norm_mlp"][i], p["mlp_w1"], p["mlp_w2"], i, p["norm_final"],
                 final_norm=(i == DEPTH - 1))
    return (x.reshape(bsz, tlen, d), jnp.stack(s5_re_out), jnp.stack(s5_im_out),
            jnp.stack(conv_out), ssm_out)


def kernel(x_prompt, x_sample, state_s5_re, state_s5_im, state_ssd_conv, state_ssd_ssm, norm_mix, norm_mlp, norm_final, s5_lam_re, s5_lam_im, s5_log_dt, s5_b_re, s5_b_im, s5_c_re, s5_c_im, s5_d, s5_w_glu, ssd_w_in, ssd_conv_w, ssd_conv_b, ssd_dt_bias, ssd_a_log, ssd_d, ssd_norm, ssd_w_out, mlp_w1, mlp_w2):
    n_s5 = s5_lam_re.shape[0]
    n_ssd = ssd_w_in.shape[0]
    p = dict(
        norm_mix=norm_mix, norm_mlp=norm_mlp, norm_final=norm_final,
        s5=[_s5_params(s5_lam_re[j], s5_lam_im[j], s5_log_dt[j], s5_b_re[j], s5_b_im[j],
                       s5_c_re[j], s5_c_im[j]) for j in range(n_s5)],
        s5_d=s5_d, s5_w_glu=s5_w_glu.astype(BF16),
        ssd_w_in=[_ssd_in_weight(ssd_w_in[j]) for j in range(n_ssd)],
        ssd_conv_w=ssd_conv_w, ssd_conv_b=ssd_conv_b, ssd_dt_bias=ssd_dt_bias, ssd_a_log=ssd_a_log,
        ssd_d=ssd_d, ssd_norm=ssd_norm, ssd_w_out=ssd_w_out.astype(BF16),
        mlp_w1=mlp_w1.astype(BF16), mlp_w2=mlp_w2.astype(BF16),
    )
    bp = x_prompt.shape[0]
    zeros = lambda shape: jnp.zeros(shape, F32)
    y_p, s5_re_p, s5_im_p, conv_p, ssm_p = _run(
        x_prompt,
        zeros((n_s5, bp, S5_GROUPS, S5_STATE)), zeros((n_s5, bp, S5_GROUPS, S5_STATE)),
        zeros((n_ssd, bp, SSD_CONV - 1, SSD_CONV_DIM)),
        zeros((n_ssd, bp, SSD_HEADS, SSD_HEAD_DIM, SSD_STATE)), p)
    y_s, s5_re_s, s5_im_s, conv_s, ssm_s = _run(
        x_sample, state_s5_re, state_s5_im, state_ssd_conv, state_ssd_ssm, p)
    return (y_p, y_s, s5_re_p, s5_im_p, conv_p, ssm_p, s5_re_s, s5_im_s, conv_s, ssm_s)
```

```python
import functools

import jax
import jax.numpy as jnp
from jax import lax
from jax.experimental import pallas as pl
from jax.experimental.pallas import tpu as pltpu

F32 = jnp.float32
BF16 = jnp.bfloat16
HIGHEST = lax.Precision.HIGHEST

D_MODEL = 2048
DEPTH = 4
RMS_EPS = 1e-5
S5_GROUP = 16
S5_GROUPS = D_MODEL // S5_GROUP
S5_STATE = 64
S5_SLABS = 8
S5_SLAB_CH = D_MODEL // S5_SLABS
S5_SLAB_ST = (S5_GROUPS // S5_SLABS) * S5_STATE
SSD_INNER = 2 * D_MODEL
SSD_HEAD_DIM = 64
SSD_HEADS = SSD_INNER // SSD_HEAD_DIM
SSD_GROUPS = 8
SSD_STATE = 128
SSD_CONV = 4
SSD_CONV_DIM = SSD_INNER + 2 * SSD_GROUPS * SSD_STATE
SSD_GROUP_CH = SSD_INNER // SSD_GROUPS
LANES = 128
SUBLANES = 8
D_FF = 4 * D_MODEL

VMEM_LIMIT = 56 * 1024 * 1024

NT_DIMS = (((1,), (1,)), ((), ()))
LOG2_E = 1.4426950408889634


def _cparams(n_axes):
    return pltpu.CompilerParams(
        dimension_semantics=("arbitrary",) * n_axes, vmem_limit_bytes=VMEM_LIMIT)


def _rms(x, g):
    ms = jnp.mean(x * x, axis=-1, keepdims=True)
    return x * lax.rsqrt(ms + RMS_EPS) * g


def _silu(x):
    h = 0.5 * x
    return h + h * jnp.tanh(h)


def _row_tile(m, want):
    t = min(m, want)
    assert m % t == 0, (m, t)
    return t


def _resident(shape):
    return pl.BlockSpec(shape, lambda i: (0,) * len(shape), pipeline_mode=pl.Buffered(1))


def _norm_matmul_kernel(x_ref, g_ref, *refs):
    *w_refs, o_ref = refs
    u = _rms(x_ref[...], g_ref[...]).astype(BF16)
    c0 = 0
    for w_ref in w_refs:
        c1 = c0 + w_ref.shape[1]
        o_ref[:, c0:c1] = jnp.dot(u, w_ref[...], preferred_element_type=F32)
        c0 = c1


def _layer_resident(w, layer):
    return pl.BlockSpec((None,) + w.shape[1:], lambda i: (layer,) + (0,) * (w.ndim - 1),
                        pipeline_mode=pl.Buffered(1))


def _norm_matmul(x, g, ws, layer, tm=256):
    m, d = x.shape
    n = sum(w.shape[2] for w in ws)
    tm = _row_tile(m, tm)
    return pl.pallas_call(
        _norm_matmul_kernel,
        grid=(m // tm,),
        in_specs=[pl.BlockSpec((tm, d), lambda i: (i, 0)), _resident((1, d))]
                 + [_layer_resident(w, layer) for w in ws],
        out_specs=pl.BlockSpec((tm, n), lambda i: (i, 0)),
        out_shape=jax.ShapeDtypeStruct((m, n), F32),
        compiler_params=_cparams(1),
        name="norm_matmul",
    )(x, g.reshape(1, d), *ws)


def _causal_conv_silu(staged, cw, cb, rows):
    xc = cb
    for kk in range(SSD_CONV):
        lag = SSD_CONV - 1 - kk
        src = staged if lag == 0 else pltpu.roll(staged, lag, axis=0)
        xc = xc + cw[kk:kk + 1, :] * src[SUBLANES:SUBLANES + rows, :]
    return _silu(xc)


MLP_SPLITS = 2


def _mlp_part_kernel(*refs, first, final_norm):
    if first:
        x_ref, g_ref, w1_ref, w2_ref, gf_ref, o_ref = refs
        acc_ref = x_ref
    else:
        x_ref, acc_ref, g_ref, w1_ref, w2_ref, gf_ref, o_ref = refs
    u = _rms(x_ref[...], g_ref[...]).astype(BF16)
    a = jnp.dot(u, w1_ref[...], preferred_element_type=F32)
    a = jnp.maximum(a, 0.0)
    a = (a * a).astype(BF16)
    o = acc_ref[...] + jnp.dot(a, w2_ref[...], preferred_element_type=F32)
    if final_norm:
        o = _rms(o, gf_ref[...])
    o_ref[...] = o


def _mlp(x, g, w1, w2, layer, gf, final_norm, tm=256):
    m, d = x.shape
    f = w1.shape[2]
    fs = f // MLP_SPLITS
    tm = _row_tile(m, tm)
    rows = pl.BlockSpec((tm, d), lambda i: (i, 0))
    acc = x
    for h in range(MLP_SPLITS):
        first = h == 0
        last = h == MLP_SPLITS - 1
        w_specs = [pl.BlockSpec((None, d, fs), lambda i, h=h: (layer, 0, h),
                                pipeline_mode=pl.Buffered(1)),
                   pl.BlockSpec((None, fs, d), lambda i, h=h: (layer, h, 0),
                                pipeline_mode=pl.Buffered(1))]
        acc = pl.pallas_call(
            functools.partial(_mlp_part_kernel, first=first, final_norm=final_norm and last),
            grid=(m // tm,),
            in_specs=([rows] if first else [rows, rows]) + [_resident((1, d))] + w_specs
                     + [_resident((1, d))],
            out_specs=rows,
            out_shape=jax.ShapeDtypeStruct((m, d), F32),
            compiler_params=_cparams(1),
            name="mlp",
        )(*([x] if first else [x, acc]), g.reshape(1, d), w1, w2, gf.reshape(1, d))
    return acc


def _matmul_residual_kernel(x_ref, a_ref, w_ref, o_ref):
    o_ref[...] = x_ref[...] + jnp.dot(a_ref[...], w_ref[...], preferred_element_type=F32)


def _matmul_residual(x, a, w, layer, tm=512):
    m, n = x.shape
    kdim = a.shape[1]
    tm = _row_tile(m, tm)
    return pl.pallas_call(
        _matmul_residual_kernel,
        grid=(m // tm,),
        in_specs=[pl.BlockSpec((tm, n), lambda i: (i, 0)),
                  pl.BlockSpec((tm, kdim), lambda i: (i, 0)),
                  _layer_resident(w, layer)],
        out_specs=pl.BlockSpec((tm, n), lambda i: (i, 0)),
        out_shape=jax.ShapeDtypeStruct((m, n), F32),
        compiler_params=_cparams(1),
        name="matmul_residual",
    )(x, a, w)


def _glu_residual_kernel(x_ref, a_ref, w_ref, o_ref):
    a = a_ref[...]
    n = o_ref.shape[1]
    ga = jnp.dot(a, w_ref[:, :n], preferred_element_type=F32)
    gb = jnp.dot(a, w_ref[:, n:], preferred_element_type=F32)
    o_ref[...] = x_ref[...] + ga * jax.nn.sigmoid(gb)


def _glu_residual(x, a, w, layer, tm=512):
    m, n = x.shape
    kdim = a.shape[1]
    tm = _row_tile(m, tm)
    return pl.pallas_call(
        _glu_residual_kernel,
        grid=(m // tm,),
        in_specs=[pl.BlockSpec((tm, n), lambda i: (i, 0)),
                  pl.BlockSpec((tm, kdim), lambda i: (i, 0)),
                  _layer_resident(w, layer)],
        out_specs=pl.BlockSpec((tm, n), lambda i: (i, 0)),
        out_shape=jax.ShapeDtypeStruct((m, n), F32),
        compiler_params=_cparams(1),
        name="glu_residual",
    )(x, a, w)


S5_SCAN_CHUNKS = 2


S5_STEP_SLABS = 2


def _s5_kernel(x_ref, gn_ref, wb_ref, wc_ref, lam_ref, d_ref, h0_ref, g_ref, hf_ref,
               u_ref, hbuf_ref, hs_ref, *, bsz, tt, nk):
    k = pl.program_id(0)
    sp = pl.program_id(1)
    ns = S5_STEP_SLABS
    wch = ns * S5_SLAB_CH

    @pl.when((k == 0) & (sp == 0))
    def _():
        hs_ref[...] = h0_ref[...]

    @pl.when(sp == 0)
    def _():
        un = _rms(x_ref[...], gn_ref[...])
        for j in range(D_MODEL // wch):
            u_ref[j] = un[:, :, j * wch:(j + 1) * wch]

    u = jnp.swapaxes(u_ref[sp], 0, 1).reshape(tt * bsz, wch)
    for i in range(ns):
        ui = u[:, i * S5_SLAB_CH:(i + 1) * S5_SLAB_CH].astype(BF16)
        hbuf_ref[i] = jnp.dot(ui, wb_ref[sp * ns + i], preferred_element_type=F32)

    cw = S5_SLAB_ST // S5_SCAN_CHUNKS
    for i in range(ns):
        slab = sp * ns + i
        for q in range(S5_SCAN_CHUNKS):
            re = slice(q * cw, (q + 1) * cw)
            im = slice(S5_SLAB_ST + q * cw, S5_SLAB_ST + (q + 1) * cw)
            lr = lam_ref[slab, 0:1, re]
            li = lam_ref[slab, 1:2, re]

            def step(t, carry, i=i, re=re, im=im, lr=lr, li=li):
                hr, hi = carry
                rows = pl.ds(pl.multiple_of(t * bsz, bsz), bsz)
                nr = lr * hr - li * hi + hbuf_ref[i, rows, re]
                ni = lr * hi + li * hr + hbuf_ref[i, rows, im]
                hbuf_ref[i, rows, re] = nr
                hbuf_ref[i, rows, im] = ni
                return nr, ni

            hr, hi = lax.fori_loop(0, tt, step, (hs_ref[slab, :, re], hs_ref[slab, :, im]), unroll=True)
            hs_ref[slab, :, re] = hr
            hs_ref[slab, :, im] = hi

    d = d_ref[sp]
    for i in range(ns):
        cs = slice(i * S5_SLAB_CH, (i + 1) * S5_SLAB_CH)
        y = jnp.dot(hbuf_ref[i].astype(BF16), wc_ref[sp * ns + i], preferred_element_type=F32)
        y = jax.nn.gelu(y + d[:, cs] * u[:, cs])
        y = jnp.swapaxes(y.reshape(tt, bsz, S5_SLAB_CH), 0, 1)
        g_ref[:, :, cs] = y.astype(BF16)

    @pl.when((k == nk - 1) & (sp == S5_SLABS // ns - 1))
    def _():
        hf_ref[...] = hs_ref[...]


def _s5_core(x3, norm_g, wb, wc, lam, d, h0):
    bsz, tlen, _ = x3.shape
    tt = min(tlen, 512 // bsz)
    rows = tt * bsz
    nk = tlen // tt
    ns = S5_STEP_SLABS
    nsp = S5_SLABS // ns
    full = lambda shape: pl.BlockSpec(shape, lambda k, s: (0,) * len(shape),
                                      pipeline_mode=pl.Buffered(1))
    return pl.pallas_call(
        functools.partial(_s5_kernel, bsz=bsz, tt=tt, nk=nk),
        grid=(nk, nsp),
        in_specs=[pl.BlockSpec((bsz, tt, D_MODEL), lambda k, s: (0, k, 0)),
                  full((1, D_MODEL)),
                  full((S5_SLABS, S5_SLAB_CH, 2 * S5_SLAB_ST)),
                  full((S5_SLABS, 2 * S5_SLAB_ST, S5_SLAB_CH)),
                  full((S5_SLABS, 2, S5_SLAB_ST)),
                  full((nsp, 1, ns * S5_SLAB_CH)),
                  full((S5_SLABS, bsz, 2 * S5_SLAB_ST))],
        out_specs=[pl.BlockSpec((bsz, tt, ns * S5_SLAB_CH), lambda k, s: (0, k, s)),
                   pl.BlockSpec((S5_SLABS, bsz, 2 * S5_SLAB_ST), lambda k, s: (0, 0, 0))],
        out_shape=[jax.ShapeDtypeStruct((bsz, tlen, D_MODEL), BF16),
                   jax.ShapeDtypeStruct((S5_SLABS, bsz, 2 * S5_SLAB_ST), F32)],
        scratch_shapes=[pltpu.VMEM((nsp, bsz, tt, ns * S5_SLAB_CH), F32),
                        pltpu.VMEM((ns, rows, 2 * S5_SLAB_ST), F32),
                        pltpu.VMEM((S5_SLABS, bsz, 2 * S5_SLAB_ST), F32)],
        compiler_params=_cparams(2),
        name="s5_core",
    )(x3, norm_g.reshape(1, D_MODEL), wb, wc, lam, d.reshape(nsp, 1, ns * S5_SLAB_CH), h0)


def _s5_params(lam_re, lam_im, log_dt, b_re, b_im, c_re, c_im):
    dt = jnp.exp(log_dt)[:, None]
    er = jnp.exp(lam_re * dt)
    lbr = er * jnp.cos(lam_im * dt)
    lbi = er * jnp.sin(lam_im * dt)
    den = lam_re * lam_re + lam_im * lam_im
    qr = ((lbr - 1.0) * lam_re + lbi * lam_im) / den
    qi = (lbi * lam_re - (lbr - 1.0) * lam_im) / den
    bbr = qr[..., None] * b_re - qi[..., None] * b_im
    bbi = qr[..., None] * b_im + qi[..., None] * b_re
    gl = S5_GROUPS // S5_SLABS
    eye = jnp.eye(gl, dtype=F32)

    def pack_b(bb):
        bb = bb.reshape(S5_SLABS, gl, S5_STATE, S5_GROUP)
        w = jnp.einsum("sgpc,gh->sgchp", bb, eye)
        return w.reshape(S5_SLABS, S5_SLAB_CH, S5_SLAB_ST)

    def pack_c(cc):
        cc = cc.reshape(S5_SLABS, gl, S5_GROUP, S5_STATE)
        w = jnp.einsum("sgcp,gh->sgphc", cc, eye)
        return w.reshape(S5_SLABS, S5_SLAB_ST, S5_SLAB_CH)

    wb = jnp.concatenate([pack_b(bbr), pack_b(bbi)], axis=2).astype(BF16)
    wc = jnp.concatenate([pack_c(c_re), pack_c(-c_im)], axis=1).astype(BF16)
    lam = jnp.stack([lbr.reshape(S5_SLABS, S5_SLAB_ST), lbi.reshape(S5_SLABS, S5_SLAB_ST)], axis=1)
    return wb, wc, lam


def _s5_state_in(h):
    bsz = h.shape[0]
    return h.reshape(bsz, S5_SLABS, S5_SLAB_ST).transpose(1, 0, 2)


def _s5_state_out(h):
    bsz = h.shape[1]
    return h.transpose(1, 0, 2).reshape(bsz, S5_GROUPS, S5_STATE)


def _s5_layer(x, bsz, tlen, h_re, h_im, norm_g, prm, d_skip, w_glu, layer):
    wb, wc, lam = prm
    h0 = jnp.concatenate([_s5_state_in(h_re), _s5_state_in(h_im)], axis=2)
    g, hf = _s5_core(x.reshape(bsz, tlen, D_MODEL), norm_g, wb, wc, lam, d_skip, h0)
    x = _glu_residual(x, g.reshape(bsz * tlen, D_MODEL), w_glu, layer)
    return x, _s5_state_out(hf[:, :, :S5_SLAB_ST]), _s5_state_out(hf[:, :, S5_SLAB_ST:])


def _softplus(x):
    return jnp.maximum(x, 0.0) + jnp.log(1.0 + jnp.exp(-jnp.abs(x)))


def _lane_bcast(v, col, width):
    return jnp.broadcast_to(v[:, col:col + 1], (v.shape[0], width))


def _ssd_kernel(xbc_ref, z0_ref, z1_ref, dt_ref, sst_ref, dtb_ref, alog_ref, dsk_ref, nw_ref,
                cst_ref, cw_ref, cb_ref, y_ref, sso_ref, cso_ref,
                st_ref, yb_ref, xp_ref, xs_ref, *, clen, nk):
    k = pl.program_id(1)

    @pl.when(k == 0)
    def _():
        st_ref[...] = sst_ref[...].reshape(SSD_INNER, SSD_STATE)
        xp_ref[0:SUBLANES, :] = cst_ref[0]

    xp_ref[SUBLANES:SUBLANES + clen, :] = xbc_ref[...]
    xs_ref[...] = _causal_conv_silu(xp_ref[...], cw_ref[...], cb_ref[...], clen)
    xp_ref[0:SUBLANES, :] = xp_ref[clen:clen + SUBLANES, :]

    bmat = xs_ref[:, SSD_INNER:SSD_INNER + SSD_GROUPS * SSD_STATE].astype(BF16)
    cmat = xs_ref[:, SSD_INNER + SSD_GROUPS * SSD_STATE:].astype(BF16)

    dt = _softplus(dt_ref[...] + dtb_ref[...])
    da = dt * (-jnp.exp(alog_ref[...]))
    ri = lax.broadcasted_iota(jnp.int32, (clen, clen), 0)
    ci = lax.broadcasted_iota(jnp.int32, (clen, clen), 1)
    causal = ri >= ci
    acum = jnp.dot(causal.astype(F32), da, precision=HIGHEST, preferred_element_type=F32)
    eye = (lax.broadcasted_iota(jnp.int32, (LANES, LANES), 0)
           == lax.broadcasted_iota(jnp.int32, (LANES, LANES), 1)).astype(F32)
    acum_t = lax.dot_general(eye, acum, NT_DIMS, precision=HIGHEST, preferred_element_type=F32)
    dt_t = lax.dot_general(eye, dt, NT_DIMS, precision=HIGHEST, preferred_element_type=F32)
    eacum = jnp.exp(acum)
    tailw = jnp.exp(acum[clen - 1:clen, :] - acum) * dt
    sdec = jnp.exp(acum_t[:, clen - 1:clen])
    acum2 = acum * LOG2_E
    acum2_t = acum_t * LOG2_E

    lo_lane = lax.broadcasted_iota(jnp.int32, (clen, LANES), 1) < SSD_HEAD_DIM
    lo_row = lax.broadcasted_iota(jnp.int32, (LANES, SSD_STATE), 0) < SSD_HEAD_DIM

    for g in range(SSD_GROUPS):
        gs = slice(g * SSD_STATE, (g + 1) * SSD_STATE)
        cg = cmat[:, gs]
        bg = bmat[:, gs]
        cbm = lax.dot_general(cg, bg, NT_DIMS, preferred_element_type=F32)
        cbm = jnp.where(causal, cbm, 0.0)
        sg = st_ref[g * SSD_GROUP_CH:(g + 1) * SSD_GROUP_CH, :].astype(BF16)
        chs = lax.dot_general(cg, sg, NT_DIMS, preferred_element_type=F32)
        for j in range(SSD_GROUP_CH // LANES):
            h0 = (g * SSD_GROUP_CH + j * LANES) // SSD_HEAD_DIM
            h1 = h0 + 1
            cs = slice(g * SSD_GROUP_CH + j * LANES, g * SSD_GROUP_CH + (j + 1) * LANES)
            xf = xs_ref[:, cs]
            xb = xf.astype(BF16)
            ys = []
            for h in (h0, h1):
                diff = jnp.minimum(_lane_bcast(acum2, h, clen) - acum2_t[h:h + 1, :], 0.0)
                w = cbm * jnp.exp2(diff) * dt_t[h:h + 1, :]
                ys.append(jnp.dot(w.astype(BF16), xb, preferred_element_type=F32))
            y = jnp.where(lo_lane, ys[0], ys[1])
            ea = jnp.where(lo_lane, _lane_bcast(eacum, h0, LANES), _lane_bcast(eacum, h1, LANES))
            y = y + chs[:, j * LANES:(j + 1) * LANES] * ea + dsk_ref[:, cs] * xf
            yb_ref[:, cs] = y
            tw = jnp.where(lo_lane, _lane_bcast(tailw, h0, LANES), _lane_bcast(tailw, h1, LANES))
            txb = (xf * tw).astype(BF16)
            ds = lax.dot_general(txb, bg, (((0,), (0,)), ((), ())), preferred_element_type=F32)
            dec = jnp.where(lo_row,
                            jnp.broadcast_to(sdec[h0:h0 + 1, :], (LANES, SSD_STATE)),
                            jnp.broadcast_to(sdec[h1:h1 + 1, :], (LANES, SSD_STATE)))
            st_ref[cs, :] = st_ref[cs, :] * dec + ds

    half = SSD_INNER // 2
    for hf, z_ref in enumerate((z0_ref, z1_ref)):
        z = z_ref[...]
        yh = yb_ref[:, hf * half:(hf + 1) * half] * _silu(z)
        for gg in range(half // SSD_GROUP_CH):
            yg = yh[:, gg * SSD_GROUP_CH:(gg + 1) * SSD_GROUP_CH]
            ms = jnp.mean(yg * yg, axis=-1, keepdims=True)
            cs = slice(hf * half + gg * SSD_GROUP_CH, hf * half + (gg + 1) * SSD_GROUP_CH)
            y_ref[:, cs] = (yg * lax.rsqrt(ms + RMS_EPS) * nw_ref[:, cs]).astype(BF16)

    @pl.when(k == nk - 1)
    def _():
        sso_ref[...] = st_ref[...].reshape(SSD_HEADS, SSD_HEAD_DIM, SSD_STATE)
        cso_ref[0] = xp_ref[0:SUBLANES, :]


def _ssd_kernel_with_acc(*refs, **kw):
    _ssd_kernel(*refs[:12], *refs[13:], **kw)


def _ssd_core(xbc, zdt, sst_all, sso_all, layer, cst, conv_w, conv_b, dt_bias, a_log, d_exp, norm_w,
              bsz, tlen):
    clen = min(tlen, 128)
    nk = tlen // clen
    zblk = SSD_INNER // 2
    dt_off = SSD_INNER // LANES
    row = lambda b, k: b * nk + k
    pad = lambda v: jnp.pad(v, (0, LANES - SSD_HEADS)).reshape(1, LANES)
    const = lambda shape: pl.BlockSpec(shape, lambda b, k: (0,) * len(shape))
    per_seq = lambda shape: pl.BlockSpec((1,) + shape, lambda b, k: (b,) + (0,) * len(shape))
    state = (SSD_HEADS, SSD_HEAD_DIM, SSD_STATE)
    layer_state = pl.BlockSpec((None, None) + state, lambda b, k: (layer, b, 0, 0, 0))
    in_specs = [pl.BlockSpec((clen, SSD_CONV_DIM), lambda b, k: (row(b, k), 0)),
                pl.BlockSpec((clen, zblk), lambda b, k: (row(b, k), 0)),
                pl.BlockSpec((clen, zblk), lambda b, k: (row(b, k), 1)),
                pl.BlockSpec((clen, LANES), lambda b, k: (row(b, k), dt_off)),
                layer_state,
                const((1, LANES)), const((1, LANES)), const((1, SSD_INNER)), const((1, SSD_INNER)),
                per_seq((SUBLANES, SSD_CONV_DIM)), const((SSD_CONV, SSD_CONV_DIM)),
                const((1, SSD_CONV_DIM))]
    args = [xbc, zdt, zdt, zdt, sst_all, pad(dt_bias), pad(a_log), d_exp.reshape(1, SSD_INNER),
            norm_w.reshape(1, SSD_INNER), cst, conv_w, conv_b.reshape(1, SSD_CONV_DIM)]
    body, aliases = _ssd_kernel, {}
    if sso_all is not None:
        body, aliases = _ssd_kernel_with_acc, {len(args): 1}
        in_specs.append(pl.BlockSpec(memory_space=pl.ANY))
        args.append(sso_all)
    return pl.pallas_call(
        functools.partial(body, clen=clen, nk=nk),
        grid=(bsz, nk),
        in_specs=in_specs,
        out_specs=[pl.BlockSpec((clen, SSD_INNER), lambda b, k: (row(b, k), 0)),
                   layer_state,
                   per_seq((SUBLANES, SSD_CONV_DIM))],
        out_shape=[jax.ShapeDtypeStruct((bsz * tlen, SSD_INNER), BF16),
                   jax.ShapeDtypeStruct(sst_all.shape, F32),
                   jax.ShapeDtypeStruct((bsz, SUBLANES, SSD_CONV_DIM), F32)],
        scratch_shapes=[pltpu.VMEM((SSD_INNER, SSD_STATE), F32),
                        pltpu.VMEM((clen, SSD_INNER), F32),
                        pltpu.VMEM((clen + SUBLANES, SSD_CONV_DIM), F32),
                        pltpu.VMEM((clen, SSD_CONV_DIM), F32)],
        input_output_aliases=aliases,
        compiler_params=_cparams(2),
        name="ssd_core",
    )(*args)


def _ssd_in_weight_kernel(w_ref, xbc_ref, z_ref, dt_ref):
    z_ref[...] = w_ref[:, :SSD_INNER].astype(BF16)
    xbc_ref[...] = w_ref[:, SSD_INNER:SSD_INNER + SSD_CONV_DIM].astype(BF16)
    dt_ref[...] = jnp.zeros_like(dt_ref)
    dt_ref[:, :SSD_HEADS] = w_ref[:, SSD_INNER + SSD_CONV_DIM:].astype(BF16)


def _ssd_in_weight(w_in, tr=256):
    nl, d, n = w_in.shape
    spec = lambda cols: pl.BlockSpec((None, tr, cols), lambda l, i: (l, i, 0))
    return pl.pallas_call(
        _ssd_in_weight_kernel,
        grid=(nl, d // tr),
        in_specs=[spec(n)],
        out_specs=[spec(SSD_CONV_DIM), spec(SSD_INNER), spec(LANES)],
        out_shape=[jax.ShapeDtypeStruct((nl, d, SSD_CONV_DIM), BF16),
                   jax.ShapeDtypeStruct((nl, d, SSD_INNER), BF16),
                   jax.ShapeDtypeStruct((nl, d, LANES), BF16)],
        compiler_params=_cparams(2),
        name="ssd_in_weight",
    )(w_in)


def _ssd_layer(x, bsz, tlen, conv_st, ssm_all, sso_all, layer, norm_g, w_in_p, conv_w, conv_b,
               dt_bias, a_log, d_skip, norm_w, w_out):
    hist = SSD_CONV - 1
    cst = jnp.pad(conv_st, ((0, 0), (SUBLANES - hist, 0), (0, 0)))
    d_exp = jnp.repeat(d_skip, SSD_HEAD_DIM)
    w_xbc, w_z, w_dt = w_in_p
    xbc = _norm_matmul(x, norm_g, [w_xbc], layer)
    zdt = _norm_matmul(x, norm_g, [w_z, w_dt], layer)
    y, sso_all, cso = _ssd_core(xbc, zdt, ssm_all, sso_all, layer, cst, conv_w, conv_b, dt_bias,
                                a_log, d_exp, norm_w, bsz, tlen)
    x = _matmul_residual(x, y, w_out, layer)
    return x, cso[:, SUBLANES - hist:], sso_all


def _run(x3, s5_re, s5_im, conv_st, ssm_st, p):
    bsz, tlen, d = x3.shape
    x = x3.reshape(bsz * tlen, d)
    s5_re_out, s5_im_out, conv_out = [], [], []
    ssm_out = None
    for i in range(DEPTH):
        j = i // 2
        if i % 2 == 0:
            x, hr, hi = _s5_layer(x, bsz, tlen, s5_re[j], s5_im[j], p["norm_mix"][i], p["s5"][j],
                                  p["s5_d"][j], p["s5_w_glu"], j)
            s5_re_out.append(hr)
            s5_im_out.append(hi)
        else:
            x, c, ssm_out = _ssd_layer(
                x, bsz, tlen, conv_st[j], ssm_st, ssm_out, j, p["norm_mix"][i], p["ssd_w_in"],
                p["ssd_conv_w"][j], p["ssd_conv_b"][j], p["ssd_dt_bias"][j], p["ssd_a_log"][j],
                p["ssd_d"][j], p["ssd_norm"][j], p["ssd_w_out"])
            conv_out.append(c)
        x = _mlp(x, p["norm_mlp"][i], p["mlp_w1"], p["mlp_w2"], i, p["norm_final"],
                 final_norm=(i == DEPTH - 1))
    return (x.reshape(bsz, tlen, d), jnp.stack(s5_re_out), jnp.stack(s5_im_out),
            jnp.stack(conv_out), ssm_out)


def kernel(x_prompt, x_sample, state_s5_re, state_s5_im, state_ssd_conv, state_ssd_ssm, norm_mix, norm_mlp, norm_final, s5_lam_re, s5_lam_im, s5_log_dt, s5_b_re, s5_b_im, s5_c_re, s5_c_im, s5_d, s5_w_glu, ssd_w_in, ssd_conv_w, ssd_conv_b, ssd_dt_bias, ssd_a_log, ssd_d, ssd_norm, ssd_w_out, mlp_w1, mlp_w2):
    n_s5 = s5_lam_re.shape[0]
    n_ssd = ssd_w_in.shape[0]
    p = dict(
        norm_mix=norm_mix, norm_mlp=norm_mlp, norm_final=norm_final,
        s5=[_s5_params(s5_lam_re[j], s5_lam_im[j], s5_log_dt[j], s5_b_re[j], s5_b_im[j],
                       s5_c_re[j], s5_c_im[j]) for j in range(n_s5)],
        s5_d=s5_d, s5_w_glu=s5_w_glu.astype(BF16),
        ssd_w_in=_ssd_in_weight(ssd_w_in),
        ssd_conv_w=ssd_conv_w, ssd_conv_b=ssd_conv_b, ssd_dt_bias=ssd_dt_bias, ssd_a_log=ssd_a_log,
        ssd_d=ssd_d, ssd_norm=ssd_norm, ssd_w_out=ssd_w_out.astype(BF16),
        mlp_w1=mlp_w1.astype(BF16), mlp_w2=mlp_w2.astype(BF16),
    )
    bp = x_prompt.shape[0]
    zeros = lambda shape: jnp.zeros(shape, F32)
    y_p, s5_re_p, s5_im_p, conv_p, ssm_p = _run(
        x_prompt,
        zeros((n_s5, bp, S5_GROUPS, S5_STATE)), zeros((n_s5, bp, S5_GROUPS, S5_STATE)),
        zeros((n_ssd, bp, SSD_CONV - 1, SSD_CONV_DIM)),
        zeros((n_ssd, bp, SSD_HEADS, SSD_HEAD_DIM, SSD_STATE)), p)
    y_s, s5_re_s, s5_im_s, conv_s, ssm_s = _run(
        x_sample, state_s5_re, state_s5_im, state_ssd_conv, state_ssd_ssm, p)
    return (y_p, y_s, s5_re_p, s5_im_p, conv_p, ssm_p, s5_re_s, s5_im_s, conv_s, ssm_s)
```

```python
import functools

import jax
import jax.numpy as jnp
from jax import lax
from jax.experimental import pallas as pl
from jax.experimental.pallas import tpu as pltpu

F32 = jnp.float32
BF16 = jnp.bfloat16
HIGHEST = lax.Precision.HIGHEST

D_MODEL = 2048
DEPTH = 4
RMS_EPS = 1e-5
S5_GROUP = 16
S5_GROUPS = D_MODEL // S5_GROUP
S5_STATE = 64
S5_SLABS = 8
S5_SLAB_CH = D_MODEL // S5_SLABS
S5_SLAB_ST = (S5_GROUPS // S5_SLABS) * S5_STATE
SSD_INNER = 2 * D_MODEL
SSD_HEAD_DIM = 64
SSD_HEADS = SSD_INNER // SSD_HEAD_DIM
SSD_GROUPS = 8
SSD_STATE = 128
SSD_CONV = 4
SSD_CONV_DIM = SSD_INNER + 2 * SSD_GROUPS * SSD_STATE
SSD_GROUP_CH = SSD_INNER // SSD_GROUPS
LANES = 128
SUBLANES = 8
D_FF = 4 * D_MODEL

VMEM_LIMIT = 56 * 1024 * 1024

NT_DIMS = (((1,), (1,)), ((), ()))
LOG2_E = 1.4426950408889634


def _cparams(n_axes):
    return pltpu.CompilerParams(
        dimension_semantics=("arbitrary",) * n_axes, vmem_limit_bytes=VMEM_LIMIT)


def _rms(x, g):
    ms = jnp.mean(x * x, axis=-1, keepdims=True)
    return x * lax.rsqrt(ms + RMS_EPS) * g


def _silu(x):
    h = 0.5 * x
    return h + h * jnp.tanh(h)


def _row_tile(m, want):
    t = min(m, want)
    assert m % t == 0, (m, t)
    return t


def _resident(shape):
    return pl.BlockSpec(shape, lambda i: (0,) * len(shape), pipeline_mode=pl.Buffered(1))


def _norm_matmul_kernel(x_ref, g_ref, *refs):
    *w_refs, o_ref = refs
    u = _rms(x_ref[...], g_ref[...]).astype(BF16)
    c0 = 0
    for w_ref in w_refs:
        c1 = c0 + w_ref.shape[1]
        o_ref[:, c0:c1] = jnp.dot(u, w_ref[...], preferred_element_type=F32)
        c0 = c1


def _layer_resident(w, layer):
    return pl.BlockSpec((None,) + w.shape[1:], lambda i: (layer,) + (0,) * (w.ndim - 1),
                        pipeline_mode=pl.Buffered(1))


def _norm_matmul(x, g, ws, layer, tm=256):
    m, d = x.shape
    n = sum(w.shape[2] for w in ws)
    tm = _row_tile(m, tm)
    return pl.pallas_call(
        _norm_matmul_kernel,
        grid=(m // tm,),
        in_specs=[pl.BlockSpec((tm, d), lambda i: (i, 0)), _resident((1, d))]
                 + [_layer_resident(w, layer) for w in ws],
        out_specs=pl.BlockSpec((tm, n), lambda i: (i, 0)),
        out_shape=jax.ShapeDtypeStruct((m, n), F32),
        compiler_params=_cparams(1),
        name="norm_matmul",
    )(x, g.reshape(1, d), *ws)


def _causal_conv_silu(staged, cw, cb, rows):
    xc = cb
    for kk in range(SSD_CONV):
        lag = SSD_CONV - 1 - kk
        src = staged if lag == 0 else pltpu.roll(staged, lag, axis=0)
        xc = xc + cw[kk:kk + 1, :] * src[SUBLANES:SUBLANES + rows, :]
    return _silu(xc)


MLP_SPLITS = 2


def _mlp_part_kernel(*refs, first, final_norm):
    if first:
        x_ref, g_ref, w1_ref, w2_ref, gf_ref, o_ref = refs
        acc_ref = x_ref
    else:
        x_ref, acc_ref, g_ref, w1_ref, w2_ref, gf_ref, o_ref = refs
    u = _rms(x_ref[...], g_ref[...]).astype(BF16)
    a = jnp.dot(u, w1_ref[...], preferred_element_type=F32)
    a = jnp.maximum(a, 0.0)
    a = (a * a).astype(BF16)
    o = acc_ref[...] + jnp.dot(a, w2_ref[...], preferred_element_type=F32)
    if final_norm:
        o = _rms(o, gf_ref[...])
    o_ref[...] = o


def _mlp(x, g, w1, w2, layer, gf, final_norm, tm=256):
    m, d = x.shape
    f = w1.shape[2]
    fs = f // MLP_SPLITS
    tm = _row_tile(m, tm)
    rows = pl.BlockSpec((tm, d), lambda i: (i, 0))
    acc = x
    for h in range(MLP_SPLITS):
        first = h == 0
        last = h == MLP_SPLITS - 1
        w_specs = [pl.BlockSpec((None, d, fs), lambda i, h=h: (layer, 0, h),
                                pipeline_mode=pl.Buffered(1)),
                   pl.BlockSpec((None, fs, d), lambda i, h=h: (layer, h, 0),
                                pipeline_mode=pl.Buffered(1))]
        acc = pl.pallas_call(
            functools.partial(_mlp_part_kernel, first=first, final_norm=final_norm and last),
            grid=(m // tm,),
            in_specs=([rows] if first else [rows, rows]) + [_resident((1, d))] + w_specs
                     + [_resident((1, d))],
            out_specs=rows,
            out_shape=jax.ShapeDtypeStruct((m, d), F32),
            compiler_params=_cparams(1),
            name="mlp",
        )(*([x] if first else [x, acc]), g.reshape(1, d), w1, w2, gf.reshape(1, d))
    return acc


def _matmul_residual_kernel(x_ref, a_ref, w_ref, o_ref):
    o_ref[...] = x_ref[...] + jnp.dot(a_ref[...], w_ref[...], preferred_element_type=F32)


def _matmul_residual(x, a, w, layer, tm=512):
    m, n = x.shape
    kdim = a.shape[1]
    tm = _row_tile(m, tm)
    return pl.pallas_call(
        _matmul_residual_kernel,
        grid=(m // tm,),
        in_specs=[pl.BlockSpec((tm, n), lambda i: (i, 0)),
                  pl.BlockSpec((tm, kdim), lambda i: (i, 0)),
                  _layer_resident(w, layer)],
        out_specs=pl.BlockSpec((tm, n), lambda i: (i, 0)),
        out_shape=jax.ShapeDtypeStruct((m, n), F32),
        compiler_params=_cparams(1),
        name="matmul_residual",
    )(x, a, w)


def _glu_residual_kernel(x_ref, a_ref, w_ref, o_ref):
    a = a_ref[...]
    n = o_ref.shape[1]
    ga = jnp.dot(a, w_ref[:, :n], preferred_element_type=F32)
    gb = jnp.dot(a, w_ref[:, n:], preferred_element_type=F32)
    o_ref[...] = x_ref[...] + ga * jax.nn.sigmoid(gb)


def _glu_residual(x, a, w, layer, tm=512):
    m, n = x.shape
    kdim = a.shape[1]
    tm = _row_tile(m, tm)
    return pl.pallas_call(
        _glu_residual_kernel,
        grid=(m // tm,),
        in_specs=[pl.BlockSpec((tm, n), lambda i: (i, 0)),
                  pl.BlockSpec((tm, kdim), lambda i: (i, 0)),
                  _layer_resident(w, layer)],
        out_specs=pl.BlockSpec((tm, n), lambda i: (i, 0)),
        out_shape=jax.ShapeDtypeStruct((m, n), F32),
        compiler_params=_cparams(1),
        name="glu_residual",
    )(x, a, w)


S5_SCAN_CHUNKS = 2


S5_STEP_SLABS = 2


def _s5_kernel(x_ref, gn_ref, wb_ref, wc_ref, lam_ref, d_ref, h0_ref, g_ref, hf_ref,
               u_ref, hbuf_ref, hs_ref, *, bsz, tt, nk):
    k = pl.program_id(0)
    sp = pl.program_id(1)
    ns = S5_STEP_SLABS
    wch = ns * S5_SLAB_CH

    @pl.when((k == 0) & (sp == 0))
    def _():
        hs_ref[...] = h0_ref[...]

    @pl.when(sp == 0)
    def _():
        un = _rms(x_ref[...], gn_ref[...])
        for j in range(D_MODEL // wch):
            u_ref[j] = un[:, :, j * wch:(j + 1) * wch]

    u = jnp.swapaxes(u_ref[sp], 0, 1).reshape(tt * bsz, wch)
    for i in range(ns):
        ui = u[:, i * S5_SLAB_CH:(i + 1) * S5_SLAB_CH].astype(BF16)
        hbuf_ref[i] = jnp.dot(ui, wb_ref[sp * ns + i], preferred_element_type=F32)

    cw = S5_SLAB_ST // S5_SCAN_CHUNKS
    for i in range(ns):
        slab = sp * ns + i
        for q in range(S5_SCAN_CHUNKS):
            re = slice(q * cw, (q + 1) * cw)
            im = slice(S5_SLAB_ST + q * cw, S5_SLAB_ST + (q + 1) * cw)
            lr = lam_ref[slab, 0:1, re]
            li = lam_ref[slab, 1:2, re]

            def step(t, carry, i=i, re=re, im=im, lr=lr, li=li):
                hr, hi = carry
                rows = pl.ds(pl.multiple_of(t * bsz, bsz), bsz)
                nr = lr * hr - li * hi + hbuf_ref[i, rows, re]
                ni = lr * hi + li * hr + hbuf_ref[i, rows, im]
                hbuf_ref[i, rows, re] = nr
                hbuf_ref[i, rows, im] = ni
                return nr, ni

            hr, hi = lax.fori_loop(0, tt, step, (hs_ref[slab, :, re], hs_ref[slab, :, im]), unroll=True)
            hs_ref[slab, :, re] = hr
            hs_ref[slab, :, im] = hi

    d = d_ref[sp]
    for i in range(ns):
        cs = slice(i * S5_SLAB_CH, (i + 1) * S5_SLAB_CH)
        y = jnp.dot(hbuf_ref[i].astype(BF16), wc_ref[sp * ns + i], preferred_element_type=F32)
        y = jax.nn.gelu(y + d[:, cs] * u[:, cs])
        y = jnp.swapaxes(y.reshape(tt, bsz, S5_SLAB_CH), 0, 1)
        g_ref[:, :, cs] = y.astype(BF16)

    @pl.when((k == nk - 1) & (sp == S5_SLABS // ns - 1))
    def _():
        hf_ref[...] = hs_ref[...]


def _s5_core(x3, norm_g, wb, wc, lam, d, h0):
    bsz, tlen, _ = x3.shape
    tt = min(tlen, 512 // bsz)
    rows = tt * bsz
    nk = tlen // tt
    ns = S5_STEP_SLABS
    nsp = S5_SLABS // ns
    full = lambda shape: pl.BlockSpec(shape, lambda k, s: (0,) * len(shape),
                                      pipeline_mode=pl.Buffered(1))
    return pl.pallas_call(
        functools.partial(_s5_kernel, bsz=bsz, tt=tt, nk=nk),
        grid=(nk, nsp),
        in_specs=[pl.BlockSpec((bsz, tt, D_MODEL), lambda k, s: (0, k, 0)),
                  full((1, D_MODEL)),
                  full((S5_SLABS, S5_SLAB_CH, 2 * S5_SLAB_ST)),
                  full((S5_SLABS, 2 * S5_SLAB_ST, S5_SLAB_CH)),
                  full((S5_SLABS, 2, S5_SLAB_ST)),
                  full((nsp, 1, ns * S5_SLAB_CH)),
                  full((S5_SLABS, bsz, 2 * S5_SLAB_ST))],
        out_specs=[pl.BlockSpec((bsz, tt, ns * S5_SLAB_CH), lambda k, s: (0, k, s)),
                   pl.BlockSpec((S5_SLABS, bsz, 2 * S5_SLAB_ST), lambda k, s: (0, 0, 0))],
        out_shape=[jax.ShapeDtypeStruct((bsz, tlen, D_MODEL), BF16),
                   jax.ShapeDtypeStruct((S5_SLABS, bsz, 2 * S5_SLAB_ST), F32)],
        scratch_shapes=[pltpu.VMEM((nsp, bsz, tt, ns * S5_SLAB_CH), F32),
                        pltpu.VMEM((ns, rows, 2 * S5_SLAB_ST), F32),
                        pltpu.VMEM((S5_SLABS, bsz, 2 * S5_SLAB_ST), F32)],
        compiler_params=_cparams(2),
        name="s5_core",
    )(x3, norm_g.reshape(1, D_MODEL), wb, wc, lam, d.reshape(nsp, 1, ns * S5_SLAB_CH), h0)


def _s5_params(lam_re, lam_im, log_dt, b_re, b_im, c_re, c_im):
    dt = jnp.exp(log_dt)[:, None]
    er = jnp.exp(lam_re * dt)
    lbr = er * jnp.cos(lam_im * dt)
    lbi = er * jnp.sin(lam_im * dt)
    den = lam_re * lam_re + lam_im * lam_im
    qr = ((lbr - 1.0) * lam_re + lbi * lam_im) / den
    qi = (lbi * lam_re - (lbr - 1.0) * lam_im) / den
    bbr = qr[..., None] * b_re - qi[..., None] * b_im
    bbi = qr[..., None] * b_im + qi[..., None] * b_re
    gl = S5_GROUPS // S5_SLABS
    eye = jnp.eye(gl, dtype=F32)

    def pack_b(bb):
        bb = bb.reshape(S5_SLABS, gl, S5_STATE, S5_GROUP)
        w = jnp.einsum("sgpc,gh->sgchp", bb, eye)
        return w.reshape(S5_SLABS, S5_SLAB_CH, S5_SLAB_ST)

    def pack_c(cc):
        cc = cc.reshape(S5_SLABS, gl, S5_GROUP, S5_STATE)
        w = jnp.einsum("sgcp,gh->sgphc", cc, eye)
        return w.reshape(S5_SLABS, S5_SLAB_ST, S5_SLAB_CH)

    wb = jnp.concatenate([pack_b(bbr), pack_b(bbi)], axis=2).astype(BF16)
    wc = jnp.concatenate([pack_c(c_re), pack_c(-c_im)], axis=1).astype(BF16)
    lam = jnp.stack([lbr.reshape(S5_SLABS, S5_SLAB_ST), lbi.reshape(S5_SLABS, S5_SLAB_ST)], axis=1)
    return wb, wc, lam


def _s5_state_in(h):
    bsz = h.shape[0]
    return h.reshape(bsz, S5_SLABS, S5_SLAB_ST).transpose(1, 0, 2)


def _s5_state_out(h):
    bsz = h.shape[1]
    return h.transpose(1, 0, 2).reshape(bsz, S5_GROUPS, S5_STATE)


def _s5_layer(x, bsz, tlen, h_re, h_im, norm_g, prm, d_skip, w_glu, layer):
    wb, wc, lam = prm
    h0 = jnp.concatenate([_s5_state_in(h_re), _s5_state_in(h_im)], axis=2)
    g, hf = _s5_core(x.reshape(bsz, tlen, D_MODEL), norm_g, wb, wc, lam, d_skip, h0)
    x = _glu_residual(x, g.reshape(bsz * tlen, D_MODEL), w_glu, layer)
    return x, _s5_state_out(hf[:, :, :S5_SLAB_ST]), _s5_state_out(hf[:, :, S5_SLAB_ST:])


def _softplus(x):
    return jnp.maximum(x, 0.0) + jnp.log(1.0 + jnp.exp(-jnp.abs(x)))


def _lane_bcast(v, col, width):
    return jnp.broadcast_to(v[:, col:col + 1], (v.shape[0], width))


def _ssd_kernel(xbc_ref, z0_ref, z1_ref, dt_ref, sst_ref, dtb_ref, alog_ref, dsk_ref, nw_ref,
                cst_ref, cw_ref, cb_ref, y_ref, sso_ref, cso_ref,
                st_ref, yb_ref, xp_ref, xs_ref, *, clen, nk):
    k = pl.program_id(1)

    @pl.when(k == 0)
    def _():
        st_ref[...] = sst_ref[...].reshape(SSD_INNER, SSD_STATE)
        xp_ref[0:SUBLANES, :] = cst_ref[0]

    xp_ref[SUBLANES:SUBLANES + clen, :] = xbc_ref[...]
    xs_ref[...] = _causal_conv_silu(xp_ref[...], cw_ref[...], cb_ref[...], clen)
    xp_ref[0:SUBLANES, :] = xp_ref[clen:clen + SUBLANES, :]

    bmat = xs_ref[:, SSD_INNER:SSD_INNER + SSD_GROUPS * SSD_STATE].astype(BF16)
    cmat = xs_ref[:, SSD_INNER + SSD_GROUPS * SSD_STATE:].astype(BF16)

    dt = _softplus(dt_ref[...] + dtb_ref[...])
    da = dt * (-jnp.exp(alog_ref[...]))
    ri = lax.broadcasted_iota(jnp.int32, (clen, clen), 0)
    ci = lax.broadcasted_iota(jnp.int32, (clen, clen), 1)
    causal = ri >= ci
    acum = jnp.dot(causal.astype(F32), da, precision=HIGHEST, preferred_element_type=F32)
    eye = (lax.broadcasted_iota(jnp.int32, (LANES, LANES), 0)
           == lax.broadcasted_iota(jnp.int32, (LANES, LANES), 1)).astype(F32)
    acum_t = lax.dot_general(eye, acum, NT_DIMS, precision=HIGHEST, preferred_element_type=F32)
    dt_t = lax.dot_general(eye, dt, NT_DIMS, precision=HIGHEST, preferred_element_type=F32)
    eacum = jnp.exp(acum)
    tailw = jnp.exp(acum[clen - 1:clen, :] - acum) * dt
    sdec = jnp.exp(acum_t[:, clen - 1:clen])
    acum2 = acum * LOG2_E
    acum2_t = acum_t * LOG2_E

    lo_lane = lax.broadcasted_iota(jnp.int32, (clen, LANES), 1) < SSD_HEAD_DIM
    lo_row = lax.broadcasted_iota(jnp.int32, (LANES, SSD_STATE), 0) < SSD_HEAD_DIM

    for g in range(SSD_GROUPS):
        gs = slice(g * SSD_STATE, (g + 1) * SSD_STATE)
        cg = cmat[:, gs]
        bg = bmat[:, gs]
        cbm = lax.dot_general(cg, bg, NT_DIMS, preferred_element_type=F32)
        cbm = jnp.where(causal, cbm, 0.0)
        sg = st_ref[g * SSD_GROUP_CH:(g + 1) * SSD_GROUP_CH, :].astype(BF16)
        chs = lax.dot_general(cg, sg, NT_DIMS, preferred_element_type=F32)
        for j in range(SSD_GROUP_CH // LANES):
            h0 = (g * SSD_GROUP_CH + j * LANES) // SSD_HEAD_DIM
            h1 = h0 + 1
            cs = slice(g * SSD_GROUP_CH + j * LANES, g * SSD_GROUP_CH + (j + 1) * LANES)
            xf = xs_ref[:, cs]
            xb = xf.astype(BF16)
            ys = []
            for h in (h0, h1):
                diff = jnp.minimum(_lane_bcast(acum2, h, clen) - acum2_t[h:h + 1, :], 0.0)
                w = cbm * jnp.exp2(diff) * dt_t[h:h + 1, :]
                ys.append(jnp.dot(w.astype(BF16), xb, preferred_element_type=F32))
            y = jnp.where(lo_lane, ys[0], ys[1])
            ea = jnp.where(lo_lane, _lane_bcast(eacum, h0, LANES), _lane_bcast(eacum, h1, LANES))
            y = y + chs[:, j * LANES:(j + 1) * LANES] * ea + dsk_ref[:, cs] * xf
            yb_ref[:, cs] = y
            tw = jnp.where(lo_lane, _lane_bcast(tailw, h0, LANES), _lane_bcast(tailw, h1, LANES))
            txb = (xf * tw).astype(BF16)
            ds = lax.dot_general(txb, bg, (((0,), (0,)), ((), ())), preferred_element_type=F32)
            dec = jnp.where(lo_row,
                            jnp.broadcast_to(sdec[h0:h0 + 1, :], (LANES, SSD_STATE)),
                            jnp.broadcast_to(sdec[h1:h1 + 1, :], (LANES, SSD_STATE)))
            st_ref[cs, :] = st_ref[cs, :] * dec + ds

    half = SSD_INNER // 2
    for hf, z_ref in enumerate((z0_ref, z1_ref)):
        z = z_ref[...]
        yh = yb_ref[:, hf * half:(hf + 1) * half] * _silu(z)
        for gg in range(half // SSD_GROUP_CH):
            yg = yh[:, gg * SSD_GROUP_CH:(gg + 1) * SSD_GROUP_CH]
            ms = jnp.mean(yg * yg, axis=-1, keepdims=True)
            cs = slice(hf * half + gg * SSD_GROUP_CH, hf * half + (gg + 1) * SSD_GROUP_CH)
            y_ref[:, cs] = (yg * lax.rsqrt(ms + RMS_EPS) * nw_ref[:, cs]).astype(BF16)

    @pl.when(k == nk - 1)
    def _():
        sso_ref[...] = st_ref[...].reshape(SSD_HEADS, SSD_HEAD_DIM, SSD_STATE)
        cso_ref[0] = xp_ref[0:SUBLANES, :]


def _ssd_kernel_with_acc(*refs, **kw):
    _ssd_kernel(*refs[:12], *refs[13:], **kw)


def _ssd_core(xbc, zdt, sst_all, sso_all, layer, cst, conv_w, conv_b, dt_bias, a_log, d_exp, norm_w,
              bsz, tlen):
    clen = min(tlen, 128)
    nk = tlen // clen
    zblk = SSD_INNER // 2
    dt_off = SSD_INNER // LANES
    row = lambda b, k: b * nk + k
    pad = lambda v: jnp.pad(v, (0, LANES - SSD_HEADS)).reshape(1, LANES)
    const = lambda shape: pl.BlockSpec(shape, lambda b, k: (0,) * len(shape))
    per_seq = lambda shape: pl.BlockSpec((1,) + shape, lambda b, k: (b,) + (0,) * len(shape))
    state = (SSD_HEADS, SSD_HEAD_DIM, SSD_STATE)
    layer_state = pl.BlockSpec((None, None) + state, lambda b, k: (layer, b, 0, 0, 0))
    in_specs = [pl.BlockSpec((clen, SSD_CONV_DIM), lambda b, k: (row(b, k), 0)),
                pl.BlockSpec((clen, zblk), lambda b, k: (row(b, k), 0)),
                pl.BlockSpec((clen, zblk), lambda b, k: (row(b, k), 1)),
                pl.BlockSpec((clen, LANES), lambda b, k: (row(b, k), dt_off)),
                layer_state,
                const((1, LANES)), const((1, LANES)), const((1, SSD_INNER)), const((1, SSD_INNER)),
                per_seq((SUBLANES, SSD_CONV_DIM)), const((SSD_CONV, SSD_CONV_DIM)),
                const((1, SSD_CONV_DIM))]
    args = [xbc, zdt, zdt, zdt, sst_all, pad(dt_bias), pad(a_log), d_exp.reshape(1, SSD_INNER),
            norm_w.reshape(1, SSD_INNER), cst, conv_w, conv_b.reshape(1, SSD_CONV_DIM)]
    body, aliases = _ssd_kernel, {}
    if sso_all is not None:
        body, aliases = _ssd_kernel_with_acc, {len(args): 1}
        in_specs.append(pl.BlockSpec(memory_space=pl.ANY))
        args.append(sso_all)
    return pl.pallas_call(
        functools.partial(body, clen=clen, nk=nk),
        grid=(bsz, nk),
        in_specs=in_specs,
        out_specs=[pl.BlockSpec((clen, SSD_INNER), lambda b, k: (row(b, k), 0)),
                   layer_state,
                   per_seq((SUBLANES, SSD_CONV_DIM))],
        out_shape=[jax.ShapeDtypeStruct((bsz * tlen, SSD_INNER), BF16),
                   jax.ShapeDtypeStruct(sst_all.shape, F32),
                   jax.ShapeDtypeStruct((bsz, SUBLANES, SSD_CONV_DIM), F32)],
        scratch_shapes=[pltpu.VMEM((SSD_INNER, SSD_STATE), F32),
                        pltpu.VMEM((clen, SSD_INNER), F32),
                        pltpu.VMEM((clen + SUBLANES, SSD_CONV_DIM), F32),
                        pltpu.VMEM((clen, SSD_CONV_DIM), F32)],
        input_output_aliases=aliases,
        compiler_params=_cparams(2),
        name="ssd_core",
    )(*args)


W_IN_TILE = 256


def _ssd_in_weight_kernel(wt_ref, xbc_ref, z_ref, dt_ref, *, nz, nxbc):
    j = pl.program_id(1)
    w = wt_ref[...].T

    @pl.when(j < nz)
    def _():
        z_ref[...] = w.astype(BF16)

    @pl.when((j >= nz) & (j < nz + nxbc))
    def _():
        xbc_ref[...] = w.astype(BF16)

    @pl.when(j == nz + nxbc)
    def _():
        col = lax.broadcasted_iota(jnp.int32, (w.shape[0], LANES), 1)
        dt_ref[...] = jnp.where(col < SSD_HEADS, w[:, :LANES], 0.0).astype(BF16)


def _ssd_in_weight(w_in):
    nl, d, n = w_in.shape
    nz = SSD_INNER // W_IN_TILE
    nxbc = SSD_CONV_DIM // W_IN_TILE
    wt = jnp.swapaxes(w_in, 1, 2)
    out = lambda cols: (None, d, cols)
    return pl.pallas_call(
        functools.partial(_ssd_in_weight_kernel, nz=nz, nxbc=nxbc),
        grid=(nl, pl.cdiv(n, W_IN_TILE)),
        in_specs=[pl.BlockSpec((None, W_IN_TILE, d), lambda l, j: (l, j, 0))],
        out_specs=[pl.BlockSpec(out(W_IN_TILE), lambda l, j: (l, 0, jnp.clip(j - nz, 0, nxbc - 1))),
                   pl.BlockSpec(out(W_IN_TILE), lambda l, j: (l, 0, jnp.minimum(j, nz - 1))),
                   pl.BlockSpec(out(LANES), lambda l, j: (l, 0, 0))],
        out_shape=[jax.ShapeDtypeStruct((nl, d, SSD_CONV_DIM), BF16),
                   jax.ShapeDtypeStruct((nl, d, SSD_INNER), BF16),
                   jax.ShapeDtypeStruct((nl, d, LANES), BF16)],
        compiler_params=_cparams(2),
        name="ssd_in_weight",
    )(wt)


def _ssd_layer(x, bsz, tlen, conv_st, ssm_all, sso_all, layer, norm_g, w_in_p, conv_w, conv_b,
               dt_bias, a_log, d_skip, norm_w, w_out):
    hist = SSD_CONV - 1
    cst = jnp.pad(conv_st, ((0, 0), (SUBLANES - hist, 0), (0, 0)))
    d_exp = jnp.repeat(d_skip, SSD_HEAD_DIM)
    w_xbc, w_z, w_dt = w_in_p
    xbc = _norm_matmul(x, norm_g, [w_xbc], layer)
    zdt = _norm_matmul(x, norm_g, [w_z, w_dt], layer)
    y, sso_all, cso = _ssd_core(xbc, zdt, ssm_all, sso_all, layer, cst, conv_w, conv_b, dt_bias,
                                a_log, d_exp, norm_w, bsz, tlen)
    x = _matmul_residual(x, y, w_out, layer)
    return x, cso[:, SUBLANES - hist:], sso_all


def _run(x3, s5_re, s5_im, conv_st, ssm_st, p):
    bsz, tlen, d = x3.shape
    x = x3.reshape(bsz * tlen, d)
    s5_re_out, s5_im_out, conv_out = [], [], []
    ssm_out = None
    for i in range(DEPTH):
        j = i // 2
        if i % 2 == 0:
            x, hr, hi = _s5_layer(x, bsz, tlen, s5_re[j], s5_im[j], p["norm_mix"][i], p["s5"][j],
                                  p["s5_d"][j], p["s5_w_glu"], j)
            s5_re_out.append(hr)
            s5_im_out.append(hi)
        else:
            x, c, ssm_out = _ssd_layer(
                x, bsz, tlen, conv_st[j], ssm_st, ssm_out, j, p["norm_mix"][i], p["ssd_w_in"],
                p["ssd_conv_w"][j], p["ssd_conv_b"][j], p["ssd_dt_bias"][j], p["ssd_a_log"][j],
                p["ssd_d"][j], p["ssd_norm"][j], p["ssd_w_out"])
            conv_out.append(c)
        x = _mlp(x, p["norm_mlp"][i], p["mlp_w1"], p["mlp_w2"], i, p["norm_final"],
                 final_norm=(i == DEPTH - 1))
    return (x.reshape(bsz, tlen, d), jnp.stack(s5_re_out), jnp.stack(s5_im_out),
            jnp.stack(conv_out), ssm_out)


def kernel(x_prompt, x_sample, state_s5_re, state_s5_im, state_ssd_conv, state_ssd_ssm, norm_mix, norm_mlp, norm_final, s5_lam_re, s5_lam_im, s5_log_dt, s5_b_re, s5_b_im, s5_c_re, s5_c_im, s5_d, s5_w_glu, ssd_w_in, ssd_conv_w, ssd_conv_b, ssd_dt_bias, ssd_a_log, ssd_d, ssd_norm, ssd_w_out, mlp_w1, mlp_w2):
    n_s5 = s5_lam_re.shape[0]
    n_ssd = ssd_w_in.shape[0]
    p = dict(
        norm_mix=norm_mix, norm_mlp=norm_mlp, norm_final=norm_final,
        s5=[_s5_params(s5_lam_re[j], s5_lam_im[j], s5_log_dt[j], s5_b_re[j], s5_b_im[j],
                       s5_c_re[j], s5_c_im[j]) for j in range(n_s5)],
        s5_d=s5_d, s5_w_glu=s5_w_glu.astype(BF16),
        ssd_w_in=_ssd_in_weight(ssd_w_in),
        ssd_conv_w=ssd_conv_w, ssd_conv_b=ssd_conv_b, ssd_dt_bias=ssd_dt_bias, ssd_a_log=ssd_a_log,
        ssd_d=ssd_d, ssd_norm=ssd_norm, ssd_w_out=ssd_w_out.astype(BF16),
        mlp_w1=mlp_w1.astype(BF16), mlp_w2=mlp_w2.astype(BF16),
    )
    bp = x_prompt.shape[0]
    zeros = lambda shape: jnp.zeros(shape, F32)
    y_p, s5_re_p, s5_im_p, conv_p, ssm_p = _run(
        x_prompt,
        zeros((n_s5, bp, S5_GROUPS, S5_STATE)), zeros((n_s5, bp, S5_GROUPS, S5_STATE)),
        zeros((n_ssd, bp, SSD_CONV - 1, SSD_CONV_DIM)),
        zeros((n_ssd, bp, SSD_HEADS, SSD_HEAD_DIM, SSD_STATE)), p)
    y_s, s5_re_s, s5_im_s, conv_s, ssm_s = _run(
        x_sample, state_s5_re, state_s5_im, state_ssd_conv, state_ssd_ssm, p)
    return (y_p, y_s, s5_re_p, s5_im_p, conv_p, ssm_p, s5_re_s, s5_im_s, conv_s, ssm_s)
```

```python
import functools

import jax
import jax.numpy as jnp
from jax import lax
from jax.experimental import pallas as pl
from jax.experimental.pallas import tpu as pltpu

F32 = jnp.float32
BF16 = jnp.bfloat16
HIGHEST = lax.Precision.HIGHEST

D_MODEL = 2048
DEPTH = 4
RMS_EPS = 1e-5
S5_GROUP = 16
S5_GROUPS = D_MODEL // S5_GROUP
S5_STATE = 64
S5_SLABS = 8
S5_SLAB_CH = D_MODEL // S5_SLABS
S5_SLAB_ST = (S5_GROUPS // S5_SLABS) * S5_STATE
SSD_INNER = 2 * D_MODEL
SSD_HEAD_DIM = 64
SSD_HEADS = SSD_INNER // SSD_HEAD_DIM
SSD_GROUPS = 8
SSD_STATE = 128
SSD_CONV = 4
SSD_CONV_DIM = SSD_INNER + 2 * SSD_GROUPS * SSD_STATE
SSD_GROUP_CH = SSD_INNER // SSD_GROUPS
LANES = 128
SUBLANES = 8
D_FF = 4 * D_MODEL

VMEM_LIMIT = 56 * 1024 * 1024

NT_DIMS = (((1,), (1,)), ((), ()))
LOG2_E = 1.4426950408889634


def _cparams(n_axes):
    return pltpu.CompilerParams(
        dimension_semantics=("arbitrary",) * n_axes, vmem_limit_bytes=VMEM_LIMIT)


def _rms(x, g):
    ms = jnp.mean(x * x, axis=-1, keepdims=True)
    return x * lax.rsqrt(ms + RMS_EPS) * g


def _silu(x):
    h = 0.5 * x
    return h + h * jnp.tanh(h)


def _row_tile(m, want):
    t = min(m, want)
    assert m % t == 0, (m, t)
    return t


def _resident(shape):
    return pl.BlockSpec(shape, lambda i: (0,) * len(shape), pipeline_mode=pl.Buffered(1))


def _norm_matmul_kernel(x_ref, g_ref, *refs):
    *w_refs, o_ref = refs
    u = _rms(x_ref[...], g_ref[...]).astype(BF16)
    c0 = 0
    for w_ref in w_refs:
        c1 = c0 + w_ref.shape[1]
        o_ref[:, c0:c1] = jnp.dot(u, w_ref[...], preferred_element_type=F32)
        c0 = c1


def _layer_resident(w, layer):
    return pl.BlockSpec((None,) + w.shape[1:], lambda i: (layer,) + (0,) * (w.ndim - 1),
                        pipeline_mode=pl.Buffered(1))


def _norm_matmul(x, g, ws, layer, tm=256):
    m, d = x.shape
    n = sum(w.shape[2] for w in ws)
    tm = _row_tile(m, tm)
    return pl.pallas_call(
        _norm_matmul_kernel,
        grid=(m // tm,),
        in_specs=[pl.BlockSpec((tm, d), lambda i: (i, 0)), _resident((1, d))]
                 + [_layer_resident(w, layer) for w in ws],
        out_specs=pl.BlockSpec((tm, n), lambda i: (i, 0)),
        out_shape=jax.ShapeDtypeStruct((m, n), F32),
        compiler_params=_cparams(1),
        name="norm_matmul",
    )(x, g.reshape(1, d), *ws)


def _causal_conv_silu(staged, cw, cb, rows):
    xc = cb
    for kk in range(SSD_CONV):
        lag = SSD_CONV - 1 - kk
        src = staged if lag == 0 else pltpu.roll(staged, lag, axis=0)
        xc = xc + cw[kk:kk + 1, :] * src[SUBLANES:SUBLANES + rows, :]
    return _silu(xc)


MLP_SPLITS = 2


def _mlp_part_kernel(*refs, first, final_norm):
    if first:
        x_ref, g_ref, w1_ref, w2_ref, gf_ref, o_ref = refs
        acc_ref = x_ref
    else:
        x_ref, acc_ref, g_ref, w1_ref, w2_ref, gf_ref, o_ref = refs
    u = _rms(x_ref[...], g_ref[...]).astype(BF16)
    a = jnp.dot(u, w1_ref[...], preferred_element_type=F32)
    a = jnp.maximum(a, 0.0)
    a = (a * a).astype(BF16)
    o = acc_ref[...] + jnp.dot(a, w2_ref[...], preferred_element_type=F32)
    if final_norm:
        o = _rms(o, gf_ref[...])
    o_ref[...] = o


def _mlp(x, g, w1, w2, layer, gf, final_norm, tm=256):
    m, d = x.shape
    f = w1.shape[2]
    fs = f // MLP_SPLITS
    tm = _row_tile(m, tm)
    rows = pl.BlockSpec((tm, d), lambda i: (i, 0))
    acc = x
    for h in range(MLP_SPLITS):
        first = h == 0
        last = h == MLP_SPLITS - 1
        w_specs = [pl.BlockSpec((None, d, fs), lambda i, h=h: (layer, 0, h),
                                pipeline_mode=pl.Buffered(1)),
                   pl.BlockSpec((None, fs, d), lambda i, h=h: (layer, h, 0),
                                pipeline_mode=pl.Buffered(1))]
        acc = pl.pallas_call(
            functools.partial(_mlp_part_kernel, first=first, final_norm=final_norm and last),
            grid=(m // tm,),
            in_specs=([rows] if first else [rows, rows]) + [_resident((1, d))] + w_specs
                     + [_resident((1, d))],
            out_specs=rows,
            out_shape=jax.ShapeDtypeStruct((m, d), F32),
            compiler_params=_cparams(1),
            name="mlp",
        )(*([x] if first else [x, acc]), g.reshape(1, d), w1, w2, gf.reshape(1, d))
    return acc


def _matmul_residual_kernel(x_ref, a_ref, w_ref, o_ref):
    o_ref[...] = x_ref[...] + jnp.dot(a_ref[...], w_ref[...], preferred_element_type=F32)


def _matmul_residual(x, a, w, layer, tm=512):
    m, n = x.shape
    kdim = a.shape[1]
    tm = _row_tile(m, tm)
    return pl.pallas_call(
        _matmul_residual_kernel,
        grid=(m // tm,),
        in_specs=[pl.BlockSpec((tm, n), lambda i: (i, 0)),
                  pl.BlockSpec((tm, kdim), lambda i: (i, 0)),
                  _layer_resident(w, layer)],
        out_specs=pl.BlockSpec((tm, n), lambda i: (i, 0)),
        out_shape=jax.ShapeDtypeStruct((m, n), F32),
        compiler_params=_cparams(1),
        name="matmul_residual",
    )(x, a, w)


def _glu_residual_kernel(x_ref, a_ref, w_ref, o_ref):
    a = a_ref[...]
    n = o_ref.shape[1]
    ga = jnp.dot(a, w_ref[:, :n], preferred_element_type=F32)
    gb = jnp.dot(a, w_ref[:, n:], preferred_element_type=F32)
    o_ref[...] = x_ref[...] + ga * jax.nn.sigmoid(gb)


def _glu_residual(x, a, w, layer, tm=512):
    m, n = x.shape
    kdim = a.shape[1]
    tm = _row_tile(m, tm)
    return pl.pallas_call(
        _glu_residual_kernel,
        grid=(m // tm,),
        in_specs=[pl.BlockSpec((tm, n), lambda i: (i, 0)),
                  pl.BlockSpec((tm, kdim), lambda i: (i, 0)),
                  _layer_resident(w, layer)],
        out_specs=pl.BlockSpec((tm, n), lambda i: (i, 0)),
        out_shape=jax.ShapeDtypeStruct((m, n), F32),
        compiler_params=_cparams(1),
        name="glu_residual",
    )(x, a, w)


S5_SCAN_CHUNKS = 2


S5_STEP_SLABS = 2


def _s5_kernel(x_ref, gn_ref, wb_ref, wc_ref, lam_ref, d_ref, h0_ref, g_ref, hf_ref,
               u_ref, hbuf_ref, hs_ref, *, bsz, tt, nk):
    k = pl.program_id(0)
    sp = pl.program_id(1)
    ns = S5_STEP_SLABS
    wch = ns * S5_SLAB_CH

    @pl.when((k == 0) & (sp == 0))
    def _():
        hs_ref[...] = h0_ref[...]

    @pl.when(sp == 0)
    def _():
        un = _rms(x_ref[...], gn_ref[...])
        for j in range(D_MODEL // wch):
            u_ref[j] = un[:, :, j * wch:(j + 1) * wch]

    u = jnp.swapaxes(u_ref[sp], 0, 1).reshape(tt * bsz, wch)
    for i in range(ns):
        ui = u[:, i * S5_SLAB_CH:(i + 1) * S5_SLAB_CH].astype(BF16)
        hbuf_ref[i] = jnp.dot(ui, wb_ref[sp * ns + i], preferred_element_type=F32)

    cw = S5_SLAB_ST // S5_SCAN_CHUNKS
    for i in range(ns):
        slab = sp * ns + i
        for q in range(S5_SCAN_CHUNKS):
            re = slice(q * cw, (q + 1) * cw)
            im = slice(S5_SLAB_ST + q * cw, S5_SLAB_ST + (q + 1) * cw)
            lr = lam_ref[slab, 0:1, re]
            li = lam_ref[slab, 1:2, re]

            def step(t, carry, i=i, re=re, im=im, lr=lr, li=li):
                hr, hi = carry
                rows = pl.ds(pl.multiple_of(t * bsz, bsz), bsz)
                nr = lr * hr - li * hi + hbuf_ref[i, rows, re]
                ni = lr * hi + li * hr + hbuf_ref[i, rows, im]
                hbuf_ref[i, rows, re] = nr
                hbuf_ref[i, rows, im] = ni
                return nr, ni

            hr, hi = lax.fori_loop(0, tt, step, (hs_ref[slab, :, re], hs_ref[slab, :, im]), unroll=True)
            hs_ref[slab, :, re] = hr
            hs_ref[slab, :, im] = hi

    d = d_ref[sp]
    for i in range(ns):
        cs = slice(i * S5_SLAB_CH, (i + 1) * S5_SLAB_CH)
        y = jnp.dot(hbuf_ref[i].astype(BF16), wc_ref[sp * ns + i], preferred_element_type=F32)
        y = jax.nn.gelu(y + d[:, cs] * u[:, cs])
        y = jnp.swapaxes(y.reshape(tt, bsz, S5_SLAB_CH), 0, 1)
        g_ref[:, :, cs] = y.astype(BF16)

    @pl.when((k == nk - 1) & (sp == S5_SLABS // ns - 1))
    def _():
        hf_ref[...] = hs_ref[...]


def _s5_core(x3, norm_g, wb, wc, lam, d, h0):
    bsz, tlen, _ = x3.shape
    tt = min(tlen, 512 // bsz)
    rows = tt * bsz
    nk = tlen // tt
    ns = S5_STEP_SLABS
    nsp = S5_SLABS // ns
    full = lambda shape: pl.BlockSpec(shape, lambda k, s: (0,) * len(shape),
                                      pipeline_mode=pl.Buffered(1))
    return pl.pallas_call(
        functools.partial(_s5_kernel, bsz=bsz, tt=tt, nk=nk),
        grid=(nk, nsp),
        in_specs=[pl.BlockSpec((bsz, tt, D_MODEL), lambda k, s: (0, k, 0)),
                  full((1, D_MODEL)),
                  full((S5_SLABS, S5_SLAB_CH, 2 * S5_SLAB_ST)),
                  full((S5_SLABS, 2 * S5_SLAB_ST, S5_SLAB_CH)),
                  full((S5_SLABS, 2, S5_SLAB_ST)),
                  full((nsp, 1, ns * S5_SLAB_CH)),
                  full((S5_SLABS, bsz, 2 * S5_SLAB_ST))],
        out_specs=[pl.BlockSpec((bsz, tt, ns * S5_SLAB_CH), lambda k, s: (0, k, s)),
                   pl.BlockSpec((S5_SLABS, bsz, 2 * S5_SLAB_ST), lambda k, s: (0, 0, 0))],
        out_shape=[jax.ShapeDtypeStruct((bsz, tlen, D_MODEL), BF16),
                   jax.ShapeDtypeStruct((S5_SLABS, bsz, 2 * S5_SLAB_ST), F32)],
        scratch_shapes=[pltpu.VMEM((nsp, bsz, tt, ns * S5_SLAB_CH), F32),
                        pltpu.VMEM((ns, rows, 2 * S5_SLAB_ST), F32),
                        pltpu.VMEM((S5_SLABS, bsz, 2 * S5_SLAB_ST), F32)],
        compiler_params=_cparams(2),
        name="s5_core",
    )(x3, norm_g.reshape(1, D_MODEL), wb, wc, lam, d.reshape(nsp, 1, ns * S5_SLAB_CH), h0)


def _s5_params(lam_re, lam_im, log_dt, b_re, b_im, c_re, c_im):
    dt = jnp.exp(log_dt)[:, None]
    er = jnp.exp(lam_re * dt)
    lbr = er * jnp.cos(lam_im * dt)
    lbi = er * jnp.sin(lam_im * dt)
    den = lam_re * lam_re + lam_im * lam_im
    qr = ((lbr - 1.0) * lam_re + lbi * lam_im) / den
    qi = (lbi * lam_re - (lbr - 1.0) * lam_im) / den
    bbr = qr[..., None] * b_re - qi[..., None] * b_im
    bbi = qr[..., None] * b_im + qi[..., None] * b_re
    gl = S5_GROUPS // S5_SLABS
    eye = jnp.eye(gl, dtype=F32)

    def pack_b(bb):
        bb = bb.reshape(S5_SLABS, gl, S5_STATE, S5_GROUP)
        w = jnp.einsum("sgpc,gh->sgchp", bb, eye)
        return w.reshape(S5_SLABS, S5_SLAB_CH, S5_SLAB_ST)

    def pack_c(cc):
        cc = cc.reshape(S5_SLABS, gl, S5_GROUP, S5_STATE)
        w = jnp.einsum("sgcp,gh->sgphc", cc, eye)
        return w.reshape(S5_SLABS, S5_SLAB_ST, S5_SLAB_CH)

    wb = jnp.concatenate([pack_b(bbr), pack_b(bbi)], axis=2).astype(BF16)
    wc = jnp.concatenate([pack_c(c_re), pack_c(-c_im)], axis=1).astype(BF16)
    lam = jnp.stack([lbr.reshape(S5_SLABS, S5_SLAB_ST), lbi.reshape(S5_SLABS, S5_SLAB_ST)], axis=1)
    return wb, wc, lam


def _s5_state_in(h):
    bsz = h.shape[0]
    return h.reshape(bsz, S5_SLABS, S5_SLAB_ST).transpose(1, 0, 2)


def _s5_state_out(h):
    bsz = h.shape[1]
    return h.transpose(1, 0, 2).reshape(bsz, S5_GROUPS, S5_STATE)


def _s5_layer(x, bsz, tlen, h_re, h_im, norm_g, prm, d_skip, w_glu, layer):
    wb, wc, lam = prm
    h0 = jnp.concatenate([_s5_state_in(h_re), _s5_state_in(h_im)], axis=2)
    g, hf = _s5_core(x.reshape(bsz, tlen, D_MODEL), norm_g, wb, wc, lam, d_skip, h0)
    x = _glu_residual(x, g.reshape(bsz * tlen, D_MODEL), w_glu, layer)
    return x, _s5_state_out(hf[:, :, :S5_SLAB_ST]), _s5_state_out(hf[:, :, S5_SLAB_ST:])


def _softplus(x):
    return jnp.maximum(x, 0.0) + jnp.log(1.0 + jnp.exp(-jnp.abs(x)))


def _lane_bcast(v, col, width):
    return jnp.broadcast_to(v[:, col:col + 1], (v.shape[0], width))


def _split3(x):
    hi = x.astype(BF16)
    r = x - hi.astype(F32)
    mid = r.astype(BF16)
    lo = (r - mid.astype(F32)).astype(BF16)
    return jnp.concatenate([hi, mid, lo], axis=1)


def _ssd_kernel(xbc_ref, z0_ref, z1_ref, dt_ref, sst_ref, dtb_ref, alog_ref, dsk_ref, nw_ref,
                cst_ref, cw_ref, cb_ref, hx_ref, y_ref, sso_ref, cso_ref,
                st_ref, yb_ref, xp_ref, xs_ref, *, clen, nk):
    k = pl.program_id(1)

    @pl.when(k == 0)
    def _():
        st_ref[...] = sst_ref[...].reshape(SSD_INNER, SSD_STATE)
        xp_ref[0:SUBLANES, :] = cst_ref[0]

    xp_ref[SUBLANES:SUBLANES + clen, :] = xbc_ref[...]
    xs_ref[...] = _causal_conv_silu(xp_ref[...], cw_ref[...], cb_ref[...], clen)
    xp_ref[0:SUBLANES, :] = xp_ref[clen:clen + SUBLANES, :]

    bmat = xs_ref[:, SSD_INNER:SSD_INNER + SSD_GROUPS * SSD_STATE].astype(BF16)
    cmat = xs_ref[:, SSD_INNER + SSD_GROUPS * SSD_STATE:].astype(BF16)

    dt = _softplus(dt_ref[...] + dtb_ref[...])
    da = dt * (-jnp.exp(alog_ref[...]))
    ri = lax.broadcasted_iota(jnp.int32, (clen, clen), 0)
    ci = lax.broadcasted_iota(jnp.int32, (clen, clen), 1)
    causal = ri >= ci
    acum = jnp.dot(causal.astype(F32), da, precision=HIGHEST, preferred_element_type=F32)
    eye = (lax.broadcasted_iota(jnp.int32, (LANES, LANES), 0)
           == lax.broadcasted_iota(jnp.int32, (LANES, LANES), 1)).astype(F32)
    acum_t = lax.dot_general(eye, acum, NT_DIMS, precision=HIGHEST, preferred_element_type=F32)
    dt_t = lax.dot_general(eye, dt, NT_DIMS, precision=HIGHEST, preferred_element_type=F32)
    eacum = jnp.exp(acum)
    tailw = jnp.exp(acum[clen - 1:clen, :] - acum) * dt
    sdec = jnp.exp(acum_t[:, clen - 1:clen])
    acum2 = acum * LOG2_E
    acum2_t = acum_t * LOG2_E

    lo_lane = lax.broadcasted_iota(jnp.int32, (clen, LANES), 1) < SSD_HEAD_DIM
    ea_all = jnp.dot(_split3(eacum), hx_ref[...], preferred_element_type=F32)
    tw_all = jnp.dot(_split3(tailw), hx_ref[...], preferred_element_type=F32)
    lo_row = lax.broadcasted_iota(jnp.int32, (LANES, SSD_STATE), 0) < SSD_HEAD_DIM

    for g in range(SSD_GROUPS):
        gs = slice(g * SSD_STATE, (g + 1) * SSD_STATE)
        cg = cmat[:, gs]
        bg = bmat[:, gs]
        cbm = lax.dot_general(cg, bg, NT_DIMS, preferred_element_type=F32)
        cbm = jnp.where(causal, cbm, 0.0)
        sg = st_ref[g * SSD_GROUP_CH:(g + 1) * SSD_GROUP_CH, :].astype(BF16)
        chs = lax.dot_general(cg, sg, NT_DIMS, preferred_element_type=F32)
        for j in range(SSD_GROUP_CH // LANES):
            h0 = (g * SSD_GROUP_CH + j * LANES) // SSD_HEAD_DIM
            h1 = h0 + 1
            cs = slice(g * SSD_GROUP_CH + j * LANES, g * SSD_GROUP_CH + (j + 1) * LANES)
            xf = xs_ref[:, cs]
            xb = xf.astype(BF16)
            ys = []
            for h in (h0, h1):
                diff = jnp.minimum(_lane_bcast(acum2, h, clen) - acum2_t[h:h + 1, :], 0.0)
                w = cbm * jnp.exp2(diff) * dt_t[h:h + 1, :]
                ys.append(jnp.dot(w.astype(BF16), xb, preferred_element_type=F32))
            y = jnp.where(lo_lane, ys[0], ys[1])
            y = y + chs[:, j * LANES:(j + 1) * LANES] * ea_all[:, cs] + dsk_ref[:, cs] * xf
            yb_ref[:, cs] = y
            txb = (xf * tw_all[:, cs]).astype(BF16)
            ds = lax.dot_general(txb, bg, (((0,), (0,)), ((), ())), preferred_element_type=F32)
            dec = jnp.where(lo_row,
                            jnp.broadcast_to(sdec[h0:h0 + 1, :], (LANES, SSD_STATE)),
                            jnp.broadcast_to(sdec[h1:h1 + 1, :], (LANES, SSD_STATE)))
            st_ref[cs, :] = st_ref[cs, :] * dec + ds

    half = SSD_INNER // 2
    for hf, z_ref in enumerate((z0_ref, z1_ref)):
        z = z_ref[...]
        yh = yb_ref[:, hf * half:(hf + 1) * half] * _silu(z)
        for gg in range(half // SSD_GROUP_CH):
            yg = yh[:, gg * SSD_GROUP_CH:(gg + 1) * SSD_GROUP_CH]
            ms = jnp.mean(yg * yg, axis=-1, keepdims=True)
            cs = slice(hf * half + gg * SSD_GROUP_CH, hf * half + (gg + 1) * SSD_GROUP_CH)
            y_ref[:, cs] = (yg * lax.rsqrt(ms + RMS_EPS) * nw_ref[:, cs]).astype(BF16)

    @pl.when(k == nk - 1)
    def _():
        sso_ref[...] = st_ref[...].reshape(SSD_HEADS, SSD_HEAD_DIM, SSD_STATE)
        cso_ref[0] = xp_ref[0:SUBLANES, :]


def _ssd_kernel_with_acc(*refs, **kw):
    _ssd_kernel(*refs[:13], *refs[14:], **kw)


def _ssd_core(xbc, zdt, sst_all, sso_all, layer, cst, conv_w, conv_b, dt_bias, a_log, d_exp, norm_w,
              bsz, tlen):
    clen = min(tlen, 128)
    nk = tlen // clen
    zblk = SSD_INNER // 2
    dt_off = SSD_INNER // LANES
    row = lambda b, k: b * nk + k
    pad = lambda v: jnp.pad(v, (0, LANES - SSD_HEADS)).reshape(1, LANES)
    const = lambda shape: pl.BlockSpec(shape, lambda b, k: (0,) * len(shape))
    per_seq = lambda shape: pl.BlockSpec((1,) + shape, lambda b, k: (b,) + (0,) * len(shape))
    state = (SSD_HEADS, SSD_HEAD_DIM, SSD_STATE)
    layer_state = pl.BlockSpec((None, None) + state, lambda b, k: (layer, b, 0, 0, 0))
    in_specs = [pl.BlockSpec((clen, SSD_CONV_DIM), lambda b, k: (row(b, k), 0)),
                pl.BlockSpec((clen, zblk), lambda b, k: (row(b, k), 0)),
                pl.BlockSpec((clen, zblk), lambda b, k: (row(b, k), 1)),
                pl.BlockSpec((clen, LANES), lambda b, k: (row(b, k), dt_off)),
                layer_state,
                const((1, LANES)), const((1, LANES)), const((1, SSD_INNER)), const((1, SSD_INNER)),
                per_seq((SUBLANES, SSD_CONV_DIM)), const((SSD_CONV, SSD_CONV_DIM)),
                const((1, SSD_CONV_DIM)), const((3 * LANES, SSD_INNER))]
    head_of = jnp.arange(SSD_INNER, dtype=jnp.int32) // SSD_HEAD_DIM
    hx = (jnp.arange(LANES, dtype=jnp.int32)[:, None] == head_of[None, :]).astype(BF16)
    args = [xbc, zdt, zdt, zdt, sst_all, pad(dt_bias), pad(a_log), d_exp.reshape(1, SSD_INNER),
            norm_w.reshape(1, SSD_INNER), cst, conv_w, conv_b.reshape(1, SSD_CONV_DIM),
            jnp.concatenate([hx, hx, hx], axis=0)]
    body, aliases = _ssd_kernel, {}
    if sso_all is not None:
        body, aliases = _ssd_kernel_with_acc, {len(args): 1}
        in_specs.append(pl.BlockSpec(memory_space=pl.ANY))
        args.append(sso_all)
    return pl.pallas_call(
        functools.partial(body, clen=clen, nk=nk),
        grid=(bsz, nk),
        in_specs=in_specs,
        out_specs=[pl.BlockSpec((clen, SSD_INNER), lambda b, k: (row(b, k), 0)),
                   layer_state,
                   per_seq((SUBLANES, SSD_CONV_DIM))],
        out_shape=[jax.ShapeDtypeStruct((bsz * tlen, SSD_INNER), BF16),
                   jax.ShapeDtypeStruct(sst_all.shape, F32),
                   jax.ShapeDtypeStruct((bsz, SUBLANES, SSD_CONV_DIM), F32)],
        scratch_shapes=[pltpu.VMEM((SSD_INNER, SSD_STATE), F32),
                        pltpu.VMEM((clen, SSD_INNER), F32),
                        pltpu.VMEM((clen + SUBLANES, SSD_CONV_DIM), F32),
                        pltpu.VMEM((clen, SSD_CONV_DIM), F32)],
        input_output_aliases=aliases,
        compiler_params=_cparams(2),
        name="ssd_core",
    )(*args)


W_IN_TILE = 256


def _ssd_in_weight_kernel(wt_ref, xbc_ref, z_ref, dt_ref, *, nz, nxbc):
    j = pl.program_id(1)
    w = wt_ref[...].T

    @pl.when(j < nz)
    def _():
        z_ref[...] = w.astype(BF16)

    @pl.when((j >= nz) & (j < nz + nxbc))
    def _():
        xbc_ref[...] = w.astype(BF16)

    @pl.when(j == nz + nxbc)
    def _():
        col = lax.broadcasted_iota(jnp.int32, (w.shape[0], LANES), 1)
        dt_ref[...] = jnp.where(col < SSD_HEADS, w[:, :LANES], 0.0).astype(BF16)


def _ssd_in_weight(w_in):
    nl, d, n = w_in.shape
    nz = SSD_INNER // W_IN_TILE
    nxbc = SSD_CONV_DIM // W_IN_TILE
    wt = jnp.swapaxes(w_in, 1, 2)
    out = lambda cols: (None, d, cols)
    return pl.pallas_call(
        functools.partial(_ssd_in_weight_kernel, nz=nz, nxbc=nxbc),
        grid=(nl, pl.cdiv(n, W_IN_TILE)),
        in_specs=[pl.BlockSpec((None, W_IN_TILE, d), lambda l, j: (l, j, 0))],
        out_specs=[pl.BlockSpec(out(W_IN_TILE), lambda l, j: (l, 0, jnp.clip(j - nz, 0, nxbc - 1))),
                   pl.BlockSpec(out(W_IN_TILE), lambda l, j: (l, 0, jnp.minimum(j, nz - 1))),
                   pl.BlockSpec(out(LANES), lambda l, j: (l, 0, 0))],
        out_shape=[jax.ShapeDtypeStruct((nl, d, SSD_CONV_DIM), BF16),
                   jax.ShapeDtypeStruct((nl, d, SSD_INNER), BF16),
                   jax.ShapeDtypeStruct((nl, d, LANES), BF16)],
        compiler_params=_cparams(2),
        name="ssd_in_weight",
    )(wt)


def _ssd_layer(x, bsz, tlen, conv_st, ssm_all, sso_all, layer, norm_g, w_in_p, conv_w, conv_b,
               dt_bias, a_log, d_skip, norm_w, w_out):
    hist = SSD_CONV - 1
    cst = jnp.pad(conv_st, ((0, 0), (SUBLANES - hist, 0), (0, 0)))
    d_exp = jnp.repeat(d_skip, SSD_HEAD_DIM)
    w_xbc, w_z, w_dt = w_in_p
    xbc = _norm_matmul(x, norm_g, [w_xbc], layer)
    zdt = _norm_matmul(x, norm_g, [w_z, w_dt], layer)
    y, sso_all, cso = _ssd_core(xbc, zdt, ssm_all, sso_all, layer, cst, conv_w, conv_b, dt_bias,
                                a_log, d_exp, norm_w, bsz, tlen)
    x = _matmul_residual(x, y, w_out, layer)
    return x, cso[:, SUBLANES - hist:], sso_all


def _run(x3, s5_re, s5_im, conv_st, ssm_st, p):
    bsz, tlen, d = x3.shape
    x = x3.reshape(bsz * tlen, d)
    s5_re_out, s5_im_out, conv_out = [], [], []
    ssm_out = None
    for i in range(DEPTH):
        j = i // 2
        if i % 2 == 0:
            x, hr, hi = _s5_layer(x, bsz, tlen, s5_re[j], s5_im[j], p["norm_mix"][i], p["s5"][j],
                                  p["s5_d"][j], p["s5_w_glu"], j)
            s5_re_out.append(hr)
            s5_im_out.append(hi)
        else:
            x, c, ssm_out = _ssd_layer(
                x, bsz, tlen, conv_st[j], ssm_st, ssm_out, j, p["norm_mix"][i], p["ssd_w_in"],
                p["ssd_conv_w"][j], p["ssd_conv_b"][j], p["ssd_dt_bias"][j], p["ssd_a_log"][j],
                p["ssd_d"][j], p["ssd_norm"][j], p["ssd_w_out"])
            conv_out.append(c)
        x = _mlp(x, p["norm_mlp"][i], p["mlp_w1"], p["mlp_w2"], i, p["norm_final"],
                 final_norm=(i == DEPTH - 1))
    return (x.reshape(bsz, tlen, d), jnp.stack(s5_re_out), jnp.stack(s5_im_out),
            jnp.stack(conv_out), ssm_out)


def kernel(x_prompt, x_sample, state_s5_re, state_s5_im, state_ssd_conv, state_ssd_ssm, norm_mix, norm_mlp, norm_final, s5_lam_re, s5_lam_im, s5_log_dt, s5_b_re, s5_b_im, s5_c_re, s5_c_im, s5_d, s5_w_glu, ssd_w_in, ssd_conv_w, ssd_conv_b, ssd_dt_bias, ssd_a_log, ssd_d, ssd_norm, ssd_w_out, mlp_w1, mlp_w2):
    n_s5 = s5_lam_re.shape[0]
    n_ssd = ssd_w_in.shape[0]
    p = dict(
        norm_mix=norm_mix, norm_mlp=norm_mlp, norm_final=norm_final,
        s5=[_s5_params(s5_lam_re[j], s5_lam_im[j], s5_log_dt[j], s5_b_re[j], s5_b_im[j],
                       s5_c_re[j], s5_c_im[j]) for j in range(n_s5)],
        s5_d=s5_d, s5_w_glu=s5_w_glu.astype(BF16),
        ssd_w_in=_ssd_in_weight(ssd_w_in),
        ssd_conv_w=ssd_conv_w, ssd_conv_b=ssd_conv_b, ssd_dt_bias=ssd_dt_bias, ssd_a_log=ssd_a_log,
        ssd_d=ssd_d, ssd_norm=ssd_norm, ssd_w_out=ssd_w_out.astype(BF16),
        mlp_w1=mlp_w1.astype(BF16), mlp_w2=mlp_w2.astype(BF16),
    )
    bp = x_prompt.shape[0]
    zeros = lambda shape: jnp.zeros(shape, F32)
    y_p, s5_re_p, s5_im_p, conv_p, ssm_p = _run(
        x_prompt,
        zeros((n_s5, bp, S5_GROUPS, S5_STATE)), zeros((n_s5, bp, S5_GROUPS, S5_STATE)),
        zeros((n_ssd, bp, SSD_CONV - 1, SSD_CONV_DIM)),
        zeros((n_ssd, bp, SSD_HEADS, SSD_HEAD_DIM, SSD_STATE)), p)
    y_s, s5_re_s, s5_im_s, conv_s, ssm_s = _run(
        x_sample, state_s5_re, state_s5_im, state_ssd_conv, state_ssd_ssm, p)
    return (y_p, y_s, s5_re_p, s5_im_p, conv_p, ssm_p, s5_re_s, s5_im_s, conv_s, ssm_s)
```

```python
import functools

import jax
import jax.numpy as jnp
from jax import lax
from jax.experimental import pallas as pl
from jax.experimental.pallas import tpu as pltpu

F32 = jnp.float32
BF16 = jnp.bfloat16
HIGHEST = lax.Precision.HIGHEST

D_MODEL = 2048
DEPTH = 4
RMS_EPS = 1e-5
S5_GROUP = 16
S5_GROUPS = D_MODEL // S5_GROUP
S5_STATE = 64
S5_SLABS = 8
S5_SLAB_CH = D_MODEL // S5_SLABS
S5_SLAB_ST = (S5_GROUPS // S5_SLABS) * S5_STATE
SSD_INNER = 2 * D_MODEL
SSD_HEAD_DIM = 64
SSD_HEADS = SSD_INNER // SSD_HEAD_DIM
SSD_GROUPS = 8
SSD_STATE = 128
SSD_CONV = 4
SSD_CONV_DIM = SSD_INNER + 2 * SSD_GROUPS * SSD_STATE
SSD_GROUP_CH = SSD_INNER // SSD_GROUPS
LANES = 128
SUBLANES = 8
D_FF = 4 * D_MODEL

VMEM_LIMIT = 56 * 1024 * 1024

NT_DIMS = (((1,), (1,)), ((), ()))
LOG2_E = 1.4426950408889634


def _cparams(n_axes):
    return pltpu.CompilerParams(
        dimension_semantics=("arbitrary",) * n_axes, vmem_limit_bytes=VMEM_LIMIT)


def _rms(x, g):
    ms = jnp.mean(x * x, axis=-1, keepdims=True)
    return x * lax.rsqrt(ms + RMS_EPS) * g


def _silu(x):
    h = 0.5 * x
    return h + h * jnp.tanh(h)


def _row_tile(m, want):
    t = min(m, want)
    assert m % t == 0, (m, t)
    return t


def _resident(shape):
    return pl.BlockSpec(shape, lambda i: (0,) * len(shape), pipeline_mode=pl.Buffered(1))


def _norm_matmul_kernel(x_ref, g_ref, *refs):
    *w_refs, o_ref = refs
    u = _rms(x_ref[...], g_ref[...]).astype(BF16)
    c0 = 0
    for w_ref in w_refs:
        c1 = c0 + w_ref.shape[1]
        o_ref[:, c0:c1] = jnp.dot(u, w_ref[...], preferred_element_type=F32)
        c0 = c1


def _layer_resident(w, layer):
    return pl.BlockSpec((None,) + w.shape[1:], lambda i: (layer,) + (0,) * (w.ndim - 1),
                        pipeline_mode=pl.Buffered(1))


def _norm_matmul(x, g, ws, layer, tm=256):
    m, d = x.shape
    n = sum(w.shape[2] for w in ws)
    tm = _row_tile(m, tm)
    return pl.pallas_call(
        _norm_matmul_kernel,
        grid=(m // tm,),
        in_specs=[pl.BlockSpec((tm, d), lambda i: (i, 0)), _resident((1, d))]
                 + [_layer_resident(w, layer) for w in ws],
        out_specs=pl.BlockSpec((tm, n), lambda i: (i, 0)),
        out_shape=jax.ShapeDtypeStruct((m, n), F32),
        compiler_params=_cparams(1),
        name="norm_matmul",
    )(x, g.reshape(1, d), *ws)


def _causal_conv_silu(staged, cw, cb, rows):
    xc = cb
    for kk in range(SSD_CONV):
        lag = SSD_CONV - 1 - kk
        src = staged if lag == 0 else pltpu.roll(staged, lag, axis=0)
        xc = xc + cw[kk:kk + 1, :] * src[SUBLANES:SUBLANES + rows, :]
    return _silu(xc)


MLP_SPLITS = 2


def _mlp_part_kernel(*refs, first, final_norm):
    if first:
        x_ref, g_ref, w1_ref, w2_ref, gf_ref, o_ref = refs
        acc_ref = x_ref
    else:
        x_ref, acc_ref, g_ref, w1_ref, w2_ref, gf_ref, o_ref = refs
    u = _rms(x_ref[...], g_ref[...]).astype(BF16)
    a = jnp.dot(u, w1_ref[...], preferred_element_type=F32)
    a = jnp.maximum(a, 0.0)
    a = (a * a).astype(BF16)
    o = acc_ref[...] + jnp.dot(a, w2_ref[...], preferred_element_type=F32)
    if final_norm:
        o = _rms(o, gf_ref[...])
    o_ref[...] = o


def _mlp(x, g, w1, w2, layer, gf, final_norm, tm=256):
    m, d = x.shape
    f = w1.shape[2]
    fs = f // MLP_SPLITS
    tm = _row_tile(m, tm)
    rows = pl.BlockSpec((tm, d), lambda i: (i, 0))
    acc = x
    for h in range(MLP_SPLITS):
        first = h == 0
        last = h == MLP_SPLITS - 1
        w_specs = [pl.BlockSpec((None, d, fs), lambda i, h=h: (layer, 0, h),
                                pipeline_mode=pl.Buffered(1)),
                   pl.BlockSpec((None, fs, d), lambda i, h=h: (layer, h, 0),
                                pipeline_mode=pl.Buffered(1))]
        acc = pl.pallas_call(
            functools.partial(_mlp_part_kernel, first=first, final_norm=final_norm and last),
            grid=(m // tm,),
            in_specs=([rows] if first else [rows, rows]) + [_resident((1, d))] + w_specs
                     + [_resident((1, d))],
            out_specs=rows,
            out_shape=jax.ShapeDtypeStruct((m, d), F32),
            compiler_params=_cparams(1),
            name="mlp",
        )(*([x] if first else [x, acc]), g.reshape(1, d), w1, w2, gf.reshape(1, d))
    return acc


def _matmul_residual_kernel(x_ref, a_ref, w_ref, o_ref):
    o_ref[...] = x_ref[...] + jnp.dot(a_ref[...], w_ref[...], preferred_element_type=F32)


def _matmul_residual(x, a, w, layer, tm=512):
    m, n = x.shape
    kdim = a.shape[1]
    tm = _row_tile(m, tm)
    return pl.pallas_call(
        _matmul_residual_kernel,
        grid=(m // tm,),
        in_specs=[pl.BlockSpec((tm, n), lambda i: (i, 0)),
                  pl.BlockSpec((tm, kdim), lambda i: (i, 0)),
                  _layer_resident(w, layer)],
        out_specs=pl.BlockSpec((tm, n), lambda i: (i, 0)),
        out_shape=jax.ShapeDtypeStruct((m, n), F32),
        compiler_params=_cparams(1),
        name="matmul_residual",
    )(x, a, w)


def _glu_residual_kernel(x_ref, a_ref, w_ref, o_ref):
    a = a_ref[...]
    n = o_ref.shape[1]
    ga = jnp.dot(a, w_ref[:, :n], preferred_element_type=F32)
    gb = jnp.dot(a, w_ref[:, n:], preferred_element_type=F32)
    o_ref[...] = x_ref[...] + ga * jax.nn.sigmoid(gb)


def _glu_residual(x, a, w, layer, tm=512):
    m, n = x.shape
    kdim = a.shape[1]
    tm = _row_tile(m, tm)
    return pl.pallas_call(
        _glu_residual_kernel,
        grid=(m // tm,),
        in_specs=[pl.BlockSpec((tm, n), lambda i: (i, 0)),
                  pl.BlockSpec((tm, kdim), lambda i: (i, 0)),
                  _layer_resident(w, layer)],
        out_specs=pl.BlockSpec((tm, n), lambda i: (i, 0)),
        out_shape=jax.ShapeDtypeStruct((m, n), F32),
        compiler_params=_cparams(1),
        name="glu_residual",
    )(x, a, w)


S5_SCAN_CHUNKS = 2


S5_STEP_SLABS = 4


def _s5_kernel(x_ref, gn_ref, wb_ref, wc_ref, lam_ref, d_ref, h0_ref, g_ref, hf_ref,
               u_ref, hbuf_ref, hs_ref, *, bsz, tt, nk):
    k = pl.program_id(0)
    sp = pl.program_id(1)
    ns = S5_STEP_SLABS
    wch = ns * S5_SLAB_CH

    @pl.when((k == 0) & (sp == 0))
    def _():
        hs_ref[...] = h0_ref[...]

    @pl.when(sp == 0)
    def _():
        un = _rms(x_ref[...], gn_ref[...])
        for j in range(D_MODEL // wch):
            u_ref[j] = un[:, :, j * wch:(j + 1) * wch]

    u = jnp.swapaxes(u_ref[sp], 0, 1).reshape(tt * bsz, wch)
    for i in range(ns):
        ui = u[:, i * S5_SLAB_CH:(i + 1) * S5_SLAB_CH].astype(BF16)
        hbuf_ref[i] = jnp.dot(ui, wb_ref[sp * ns + i], preferred_element_type=F32)

    cw = S5_SLAB_ST // S5_SCAN_CHUNKS
    for i in range(ns):
        slab = sp * ns + i
        for q in range(S5_SCAN_CHUNKS):
            re = slice(q * cw, (q + 1) * cw)
            im = slice(S5_SLAB_ST + q * cw, S5_SLAB_ST + (q + 1) * cw)
            lr = lam_ref[slab, 0:1, re]
            li = lam_ref[slab, 1:2, re]

            def step(t, carry, i=i, re=re, im=im, lr=lr, li=li):
                hr, hi = carry
                rows = pl.ds(pl.multiple_of(t * bsz, bsz), bsz)
                nr = lr * hr - li * hi + hbuf_ref[i, rows, re]
                ni = lr * hi + li * hr + hbuf_ref[i, rows, im]
                hbuf_ref[i, rows, re] = nr
                hbuf_ref[i, rows, im] = ni
                return nr, ni

            hr, hi = lax.fori_loop(0, tt, step, (hs_ref[slab, :, re], hs_ref[slab, :, im]), unroll=True)
            hs_ref[slab, :, re] = hr
            hs_ref[slab, :, im] = hi

    d = d_ref[sp]
    for i in range(ns):
        cs = slice(i * S5_SLAB_CH, (i + 1) * S5_SLAB_CH)
        y = jnp.dot(hbuf_ref[i].astype(BF16), wc_ref[sp * ns + i], preferred_element_type=F32)
        y = jax.nn.gelu(y + d[:, cs] * u[:, cs])
        y = jnp.swapaxes(y.reshape(tt, bsz, S5_SLAB_CH), 0, 1)
        g_ref[:, :, cs] = y.astype(BF16)

    @pl.when((k == nk - 1) & (sp == S5_SLABS // ns - 1))
    def _():
        hf_ref[...] = hs_ref[...]


def _s5_core(x3, norm_g, wb, wc, lam, d, h0):
    bsz, tlen, _ = x3.shape
    tt = min(tlen, 512 // bsz)
    rows = tt * bsz
    nk = tlen // tt
    ns = S5_STEP_SLABS
    nsp = S5_SLABS // ns
    full = lambda shape: pl.BlockSpec(shape, lambda k, s: (0,) * len(shape),
                                      pipeline_mode=pl.Buffered(1))
    return pl.pallas_call(
        functools.partial(_s5_kernel, bsz=bsz, tt=tt, nk=nk),
        grid=(nk, nsp),
        in_specs=[pl.BlockSpec((bsz, tt, D_MODEL), lambda k, s: (0, k, 0)),
                  full((1, D_MODEL)),
                  full((S5_SLABS, S5_SLAB_CH, 2 * S5_SLAB_ST)),
                  full((S5_SLABS, 2 * S5_SLAB_ST, S5_SLAB_CH)),
                  full((S5_SLABS, 2, S5_SLAB_ST)),
                  full((nsp, 1, ns * S5_SLAB_CH)),
                  full((S5_SLABS, bsz, 2 * S5_SLAB_ST))],
        out_specs=[pl.BlockSpec((bsz, tt, ns * S5_SLAB_CH), lambda k, s: (0, k, s)),
                   pl.BlockSpec((S5_SLABS, bsz, 2 * S5_SLAB_ST), lambda k, s: (0, 0, 0))],
        out_shape=[jax.ShapeDtypeStruct((bsz, tlen, D_MODEL), BF16),
                   jax.ShapeDtypeStruct((S5_SLABS, bsz, 2 * S5_SLAB_ST), F32)],
        scratch_shapes=[pltpu.VMEM((nsp, bsz, tt, ns * S5_SLAB_CH), F32),
                        pltpu.VMEM((ns, rows, 2 * S5_SLAB_ST), F32),
                        pltpu.VMEM((S5_SLABS, bsz, 2 * S5_SLAB_ST), F32)],
        compiler_params=_cparams(2),
        name="s5_core",
    )(x3, norm_g.reshape(1, D_MODEL), wb, wc, lam, d.reshape(nsp, 1, ns * S5_SLAB_CH), h0)


def _s5_params(lam_re, lam_im, log_dt, b_re, b_im, c_re, c_im):
    dt = jnp.exp(log_dt)[:, None]
    er = jnp.exp(lam_re * dt)
    lbr = er * jnp.cos(lam_im * dt)
    lbi = er * jnp.sin(lam_im * dt)
    den = lam_re * lam_re + lam_im * lam_im
    qr = ((lbr - 1.0) * lam_re + lbi * lam_im) / den
    qi = (lbi * lam_re - (lbr - 1.0) * lam_im) / den
    bbr = qr[..., None] * b_re - qi[..., None] * b_im
    bbi = qr[..., None] * b_im + qi[..., None] * b_re
    gl = S5_GROUPS // S5_SLABS
    eye = jnp.eye(gl, dtype=F32)

    def pack_b(bb):
        bb = bb.reshape(S5_SLABS, gl, S5_STATE, S5_GROUP)
        w = jnp.einsum("sgpc,gh->sgchp", bb, eye)
        return w.reshape(S5_SLABS, S5_SLAB_CH, S5_SLAB_ST)

    def pack_c(cc):
        cc = cc.reshape(S5_SLABS, gl, S5_GROUP, S5_STATE)
        w = jnp.einsum("sgcp,gh->sgphc", cc, eye)
        return w.reshape(S5_SLABS, S5_SLAB_ST, S5_SLAB_CH)

    wb = jnp.concatenate([pack_b(bbr), pack_b(bbi)], axis=2).astype(BF16)
    wc = jnp.concatenate([pack_c(c_re), pack_c(-c_im)], axis=1).astype(BF16)
    lam = jnp.stack([lbr.reshape(S5_SLABS, S5_SLAB_ST), lbi.reshape(S5_SLABS, S5_SLAB_ST)], axis=1)
    return wb, wc, lam


def _s5_state_in(h):
    bsz = h.shape[0]
    return h.reshape(bsz, S5_SLABS, S5_SLAB_ST).transpose(1, 0, 2)


def _s5_state_out(h):
    bsz = h.shape[1]
    return h.transpose(1, 0, 2).reshape(bsz, S5_GROUPS, S5_STATE)


def _s5_layer(x, bsz, tlen, h_re, h_im, norm_g, prm, d_skip, w_glu, layer):
    wb, wc, lam = prm
    h0 = jnp.concatenate([_s5_state_in(h_re), _s5_state_in(h_im)], axis=2)
    g, hf = _s5_core(x.reshape(bsz, tlen, D_MODEL), norm_g, wb, wc, lam, d_skip, h0)
    x = _glu_residual(x, g.reshape(bsz * tlen, D_MODEL), w_glu, layer)
    return x, _s5_state_out(hf[:, :, :S5_SLAB_ST]), _s5_state_out(hf[:, :, S5_SLAB_ST:])


def _softplus(x):
    return jnp.maximum(x, 0.0) + jnp.log(1.0 + jnp.exp(-jnp.abs(x)))


def _lane_bcast(v, col, width):
    return jnp.broadcast_to(v[:, col:col + 1], (v.shape[0], width))


def _split3(x):
    hi = x.astype(BF16)
    r = x - hi.astype(F32)
    mid = r.astype(BF16)
    lo = (r - mid.astype(F32)).astype(BF16)
    return jnp.concatenate([hi, mid, lo], axis=1)


def _ssd_kernel(xbc_ref, z0_ref, z1_ref, dt_ref, sst_ref, dtb_ref, alog_ref, dsk_ref, nw_ref,
                cst_ref, cw_ref, cb_ref, hx_ref, y_ref, sso_ref, cso_ref,
                st_ref, yb_ref, xp_ref, xs_ref, *, clen, nk):
    k = pl.program_id(1)

    @pl.when(k == 0)
    def _():
        st_ref[...] = sst_ref[...].reshape(SSD_INNER, SSD_STATE)
        xp_ref[0:SUBLANES, :] = cst_ref[0]

    xp_ref[SUBLANES:SUBLANES + clen, :] = xbc_ref[...]
    xs_ref[...] = _causal_conv_silu(xp_ref[...], cw_ref[...], cb_ref[...], clen)
    xp_ref[0:SUBLANES, :] = xp_ref[clen:clen + SUBLANES, :]

    bmat = xs_ref[:, SSD_INNER:SSD_INNER + SSD_GROUPS * SSD_STATE].astype(BF16)
    cmat = xs_ref[:, SSD_INNER + SSD_GROUPS * SSD_STATE:].astype(BF16)

    dt = _softplus(dt_ref[...] + dtb_ref[...])
    da = dt * (-jnp.exp(alog_ref[...]))
    ri = lax.broadcasted_iota(jnp.int32, (clen, clen), 0)
    ci = lax.broadcasted_iota(jnp.int32, (clen, clen), 1)
    causal = ri >= ci
    acum = jnp.dot(causal.astype(F32), da, precision=HIGHEST, preferred_element_type=F32)
    eye = (lax.broadcasted_iota(jnp.int32, (LANES, LANES), 0)
           == lax.broadcasted_iota(jnp.int32, (LANES, LANES), 1)).astype(F32)
    acum_t = lax.dot_general(eye, acum, NT_DIMS, precision=HIGHEST, preferred_element_type=F32)
    dt_t = lax.dot_general(eye, dt, NT_DIMS, precision=HIGHEST, preferred_element_type=F32)
    eacum = jnp.exp(acum)
    tailw = jnp.exp(acum[clen - 1:clen, :] - acum) * dt
    sdec = jnp.exp(acum_t[:, clen - 1:clen])
    acum2 = acum * LOG2_E
    acum2_t = acum_t * LOG2_E

    lo_lane = lax.broadcasted_iota(jnp.int32, (clen, LANES), 1) < SSD_HEAD_DIM
    ea_all = jnp.dot(_split3(eacum), hx_ref[...], preferred_element_type=F32)
    tw_all = jnp.dot(_split3(tailw), hx_ref[...], preferred_element_type=F32)
    lo_row = lax.broadcasted_iota(jnp.int32, (LANES, SSD_STATE), 0) < SSD_HEAD_DIM

    for g in range(SSD_GROUPS):
        gs = slice(g * SSD_STATE, (g + 1) * SSD_STATE)
        cg = cmat[:, gs]
        bg = bmat[:, gs]
        cbm = lax.dot_general(cg, bg, NT_DIMS, preferred_element_type=F32)
        cbm = jnp.where(causal, cbm, 0.0)
        sg = st_ref[g * SSD_GROUP_CH:(g + 1) * SSD_GROUP_CH, :].astype(BF16)
        chs = lax.dot_general(cg, sg, NT_DIMS, preferred_element_type=F32)
        for j in range(SSD_GROUP_CH // LANES):
            h0 = (g * SSD_GROUP_CH + j * LANES) // SSD_HEAD_DIM
            h1 = h0 + 1
            cs = slice(g * SSD_GROUP_CH + j * LANES, g * SSD_GROUP_CH + (j + 1) * LANES)
            xf = xs_ref[:, cs]
            xb = xf.astype(BF16)
            ys = []
            for h in (h0, h1):
                diff = jnp.minimum(_lane_bcast(acum2, h, clen) - acum2_t[h:h + 1, :], 0.0)
                w = cbm * jnp.exp2(diff) * dt_t[h:h + 1, :]
                ys.append(jnp.dot(w.astype(BF16), xb, preferred_element_type=F32))
            y = jnp.where(lo_lane, ys[0], ys[1])
            y = y + chs[:, j * LANES:(j + 1) * LANES] * ea_all[:, cs] + dsk_ref[:, cs] * xf
            yb_ref[:, cs] = y
            txb = (xf * tw_all[:, cs]).astype(BF16)
            ds = lax.dot_general(txb, bg, (((0,), (0,)), ((), ())), preferred_element_type=F32)
            dec = jnp.where(lo_row,
                            jnp.broadcast_to(sdec[h0:h0 + 1, :], (LANES, SSD_STATE)),
                            jnp.broadcast_to(sdec[h1:h1 + 1, :], (LANES, SSD_STATE)))
            st_ref[cs, :] = st_ref[cs, :] * dec + ds

    half = SSD_INNER // 2
    for hf, z_ref in enumerate((z0_ref, z1_ref)):
        z = z_ref[...]
        yh = yb_ref[:, hf * half:(hf + 1) * half] * _silu(z)
        for gg in range(half // SSD_GROUP_CH):
            yg = yh[:, gg * SSD_GROUP_CH:(gg + 1) * SSD_GROUP_CH]
            ms = jnp.mean(yg * yg, axis=-1, keepdims=True)
            cs = slice(hf * half + gg * SSD_GROUP_CH, hf * half + (gg + 1) * SSD_GROUP_CH)
            y_ref[:, cs] = (yg * lax.rsqrt(ms + RMS_EPS) * nw_ref[:, cs]).astype(BF16)

    @pl.when(k == nk - 1)
    def _():
        sso_ref[...] = st_ref[...].reshape(SSD_HEADS, SSD_HEAD_DIM, SSD_STATE)
        cso_ref[0] = xp_ref[0:SUBLANES, :]


def _ssd_kernel_with_acc(*refs, **kw):
    _ssd_kernel(*refs[:13], *refs[14:], **kw)


def _ssd_core(xbc, zdt, sst_all, sso_all, layer, cst, conv_w, conv_b, dt_bias, a_log, d_exp, norm_w,
              bsz, tlen):
    clen = min(tlen, 128)
    nk = tlen // clen
    zblk = SSD_INNER // 2
    dt_off = SSD_INNER // LANES
    row = lambda b, k: b * nk + k
    pad = lambda v: jnp.pad(v, (0, LANES - SSD_HEADS)).reshape(1, LANES)
    const = lambda shape: pl.BlockSpec(shape, lambda b, k: (0,) * len(shape))
    per_seq = lambda shape: pl.BlockSpec((1,) + shape, lambda b, k: (b,) + (0,) * len(shape))
    state = (SSD_HEADS, SSD_HEAD_DIM, SSD_STATE)
    layer_state = pl.BlockSpec((None, None) + state, lambda b, k: (layer, b, 0, 0, 0))
    in_specs = [pl.BlockSpec((clen, SSD_CONV_DIM), lambda b, k: (row(b, k), 0)),
                pl.BlockSpec((clen, zblk), lambda b, k: (row(b, k), 0)),
                pl.BlockSpec((clen, zblk), lambda b, k: (row(b, k), 1)),
                pl.BlockSpec((clen, LANES), lambda b, k: (row(b, k), dt_off)),
                layer_state,
                const((1, LANES)), const((1, LANES)), const((1, SSD_INNER)), const((1, SSD_INNER)),
                per_seq((SUBLANES, SSD_CONV_DIM)), const((SSD_CONV, SSD_CONV_DIM)),
                const((1, SSD_CONV_DIM)), const((3 * LANES, SSD_INNER))]
    head_of = jnp.arange(SSD_INNER, dtype=jnp.int32) // SSD_HEAD_DIM
    hx = (jnp.arange(LANES, dtype=jnp.int32)[:, None] == head_of[None, :]).astype(BF16)
    args = [xbc, zdt, zdt, zdt, sst_all, pad(dt_bias), pad(a_log), d_exp.reshape(1, SSD_INNER),
            norm_w.reshape(1, SSD_INNER), cst, conv_w, conv_b.reshape(1, SSD_CONV_DIM),
            jnp.concatenate([hx, hx, hx], axis=0)]
    body, aliases = _ssd_kernel, {}
    if sso_all is not None:
        body, aliases = _ssd_kernel_with_acc, {len(args): 1}
        in_specs.append(pl.BlockSpec(memory_space=pl.ANY))
        args.append(sso_all)
    return pl.pallas_call(
        functools.partial(body, clen=clen, nk=nk),
        grid=(bsz, nk),
        in_specs=in_specs,
        out_specs=[pl.BlockSpec((clen, SSD_INNER), lambda b, k: (row(b, k), 0)),
                   layer_state,
                   per_seq((SUBLANES, SSD_CONV_DIM))],
        out_shape=[jax.ShapeDtypeStruct((bsz * tlen, SSD_INNER), BF16),
                   jax.ShapeDtypeStruct(sst_all.shape, F32),
                   jax.ShapeDtypeStruct((bsz, SUBLANES, SSD_CONV_DIM), F32)],
        scratch_shapes=[pltpu.VMEM((SSD_INNER, SSD_STATE), F32),
                        pltpu.VMEM((clen, SSD_INNER), F32),
                        pltpu.VMEM((clen + SUBLANES, SSD_CONV_DIM), F32),
                        pltpu.VMEM((clen, SSD_CONV_DIM), F32)],
        input_output_aliases=aliases,
        compiler_params=_cparams(2),
        name="ssd_core",
    )(*args)


W_IN_TILE = 256


def _ssd_in_weight_kernel(wt_ref, xbc_ref, z_ref, dt_ref, *, nz, nxbc):
    j = pl.program_id(1)
    w = wt_ref[...].T

    @pl.when(j < nz)
    def _():
        z_ref[...] = w.astype(BF16)

    @pl.when((j >= nz) & (j < nz + nxbc))
    def _():
        xbc_ref[...] = w.astype(BF16)

    @pl.when(j == nz + nxbc)
    def _():
        col = lax.broadcasted_iota(jnp.int32, (w.shape[0], LANES), 1)
        dt_ref[...] = jnp.where(col < SSD_HEADS, w[:, :LANES], 0.0).astype(BF16)


def _ssd_in_weight(w_in):
    nl, d, n = w_in.shape
    nz = SSD_INNER // W_IN_TILE
    nxbc = SSD_CONV_DIM // W_IN_TILE
    wt = jnp.swapaxes(w_in, 1, 2)
    out = lambda cols: (None, d, cols)
    return pl.pallas_call(
        functools.partial(_ssd_in_weight_kernel, nz=nz, nxbc=nxbc),
        grid=(nl, pl.cdiv(n, W_IN_TILE)),
        in_specs=[pl.BlockSpec((None, W_IN_TILE, d), lambda l, j: (l, j, 0))],
        out_specs=[pl.BlockSpec(out(W_IN_TILE), lambda l, j: (l, 0, jnp.clip(j - nz, 0, nxbc - 1))),
                   pl.BlockSpec(out(W_IN_TILE), lambda l, j: (l, 0, jnp.minimum(j, nz - 1))),
                   pl.BlockSpec(out(LANES), lambda l, j: (l, 0, 0))],
        out_shape=[jax.ShapeDtypeStruct((nl, d, SSD_CONV_DIM), BF16),
                   jax.ShapeDtypeStruct((nl, d, SSD_INNER), BF16),
                   jax.ShapeDtypeStruct((nl, d, LANES), BF16)],
        compiler_params=_cparams(2),
        name="ssd_in_weight",
    )(wt)


def _ssd_layer(x, bsz, tlen, conv_st, ssm_all, sso_all, layer, norm_g, w_in_p, conv_w, conv_b,
               dt_bias, a_log, d_skip, norm_w, w_out):
    hist = SSD_CONV - 1
    cst = jnp.pad(conv_st, ((0, 0), (SUBLANES - hist, 0), (0, 0)))
    d_exp = jnp.repeat(d_skip, SSD_HEAD_DIM)
    w_xbc, w_z, w_dt = w_in_p
    xbc = _norm_matmul(x, norm_g, [w_xbc], layer)
    zdt = _norm_matmul(x, norm_g, [w_z, w_dt], layer)
    y, sso_all, cso = _ssd_core(xbc, zdt, ssm_all, sso_all, layer, cst, conv_w, conv_b, dt_bias,
                                a_log, d_exp, norm_w, bsz, tlen)
    x = _matmul_residual(x, y, w_out, layer)
    return x, cso[:, SUBLANES - hist:], sso_all


def _run(x3, s5_re, s5_im, conv_st, ssm_st, p):
    bsz, tlen, d = x3.shape
    x = x3.reshape(bsz * tlen, d)
    s5_re_out, s5_im_out, conv_out = [], [], []
    ssm_out = None
    for i in range(DEPTH):
        j = i // 2
        if i % 2 == 0:
            x, hr, hi = _s5_layer(x, bsz, tlen, s5_re[j], s5_im[j], p["norm_mix"][i], p["s5"][j],
                                  p["s5_d"][j], p["s5_w_glu"], j)
            s5_re_out.append(hr)
            s5_im_out.append(hi)
        else:
            x, c, ssm_out = _ssd_layer(
                x, bsz, tlen, conv_st[j], ssm_st, ssm_out, j, p["norm_mix"][i], p["ssd_w_in"],
                p["ssd_conv_w"][j], p["ssd_conv_b"][j], p["ssd_dt_bias"][j], p["ssd_a_log"][j],
                p["ssd_d"][j], p["ssd_norm"][j], p["ssd_w_out"])
            conv_out.append(c)
        x = _mlp(x, p["norm_mlp"][i], p["mlp_w1"], p["mlp_w2"], i, p["norm_final"],
                 final_norm=(i == DEPTH - 1))
    return (x.reshape(bsz, tlen, d), jnp.stack(s5_re_out), jnp.stack(s5_im_out),
            jnp.stack(conv_out), ssm_out)


def kernel(x_prompt, x_sample, state_s5_re, state_s5_im, state_ssd_conv, state_ssd_ssm, norm_mix, norm_mlp, norm_final, s5_lam_re, s5_lam_im, s5_log_dt, s5_b_re, s5_b_im, s5_c_re, s5_c_im, s5_d, s5_w_glu, ssd_w_in, ssd_conv_w, ssd_conv_b, ssd_dt_bias, ssd_a_log, ssd_d, ssd_norm, ssd_w_out, mlp_w1, mlp_w2):
    n_s5 = s5_lam_re.shape[0]
    n_ssd = ssd_w_in.shape[0]
    p = dict(
        norm_mix=norm_mix, norm_mlp=norm_mlp, norm_final=norm_final,
        s5=[_s5_params(s5_lam_re[j], s5_lam_im[j], s5_log_dt[j], s5_b_re[j], s5_b_im[j],
                       s5_c_re[j], s5_c_im[j]) for j in range(n_s5)],
        s5_d=s5_d, s5_w_glu=s5_w_glu.astype(BF16),
        ssd_w_in=_ssd_in_weight(ssd_w_in),
        ssd_conv_w=ssd_conv_w, ssd_conv_b=ssd_conv_b, ssd_dt_bias=ssd_dt_bias, ssd_a_log=ssd_a_log,
        ssd_d=ssd_d, ssd_norm=ssd_norm, ssd_w_out=ssd_w_out.astype(BF16),
        mlp_w1=mlp_w1.astype(BF16), mlp_w2=mlp_w2.astype(BF16),
    )
    bp = x_prompt.shape[0]
    zeros = lambda shape: jnp.zeros(shape, F32)
    y_p, s5_re_p, s5_im_p, conv_p, ssm_p = _run(
        x_prompt,
        zeros((n_s5, bp, S5_GROUPS, S5_STATE)), zeros((n_s5, bp, S5_GROUPS, S5_STATE)),
        zeros((n_ssd, bp, SSD_CONV - 1, SSD_CONV_DIM)),
        zeros((n_ssd, bp, SSD_HEADS, SSD_HEAD_DIM, SSD_STATE)), p)
    y_s, s5_re_s, s5_im_s, conv_s, ssm_s = _run(
        x_sample, state_s5_re, state_s5_im, state_ssd_conv, state_ssd_ssm, p)
    return (y_p, y_s, s5_re_p, s5_im_p, conv_p, ssm_p, s5_re_s, s5_im_s, conv_s, ssm_s)
```

```python
import functools

import jax
import jax.numpy as jnp
from jax import lax
from jax.experimental import pallas as pl
from jax.experimental.pallas import tpu as pltpu

F32 = jnp.float32
BF16 = jnp.bfloat16
HIGHEST = lax.Precision.HIGHEST

D_MODEL = 2048
DEPTH = 4
RMS_EPS = 1e-5
S5_GROUP = 16
S5_GROUPS = D_MODEL // S5_GROUP
S5_STATE = 64
S5_SLABS = 8
S5_SLAB_CH = D_MODEL // S5_SLABS
S5_SLAB_ST = (S5_GROUPS // S5_SLABS) * S5_STATE
SSD_INNER = 2 * D_MODEL
SSD_HEAD_DIM = 64
SSD_HEADS = SSD_INNER // SSD_HEAD_DIM
SSD_GROUPS = 8
SSD_STATE = 128
SSD_CONV = 4
SSD_CONV_DIM = SSD_INNER + 2 * SSD_GROUPS * SSD_STATE
SSD_GROUP_CH = SSD_INNER // SSD_GROUPS
LANES = 128
SUBLANES = 8
D_FF = 4 * D_MODEL

VMEM_LIMIT = 56 * 1024 * 1024

NT_DIMS = (((1,), (1,)), ((), ()))
LOG2_E = 1.4426950408889634


def _cparams(n_axes):
    return pltpu.CompilerParams(
        dimension_semantics=("arbitrary",) * n_axes, vmem_limit_bytes=VMEM_LIMIT)


def _rms(x, g):
    ms = jnp.mean(x * x, axis=-1, keepdims=True)
    return x * lax.rsqrt(ms + RMS_EPS) * g


def _silu(x):
    h = 0.5 * x
    return h + h * jnp.tanh(h)


def _row_tile(m, want):
    t = min(m, want)
    assert m % t == 0, (m, t)
    return t


def _resident(shape):
    return pl.BlockSpec(shape, lambda i: (0,) * len(shape), pipeline_mode=pl.Buffered(1))


def _norm_matmul_kernel(x_ref, g_ref, *refs):
    *w_refs, o_ref = refs
    u = _rms(x_ref[...], g_ref[...]).astype(BF16)
    c0 = 0
    for w_ref in w_refs:
        c1 = c0 + w_ref.shape[1]
        o_ref[:, c0:c1] = jnp.dot(u, w_ref[...], preferred_element_type=F32)
        c0 = c1


def _layer_resident(w, layer):
    return pl.BlockSpec((None,) + w.shape[1:], lambda i: (layer,) + (0,) * (w.ndim - 1),
                        pipeline_mode=pl.Buffered(1))


def _norm_matmul(x, g, ws, layer, tm=256):
    m, d = x.shape
    n = sum(w.shape[2] for w in ws)
    tm = _row_tile(m, tm)
    return pl.pallas_call(
        _norm_matmul_kernel,
        grid=(m // tm,),
        in_specs=[pl.BlockSpec((tm, d), lambda i: (i, 0)), _resident((1, d))]
                 + [_layer_resident(w, layer) for w in ws],
        out_specs=pl.BlockSpec((tm, n), lambda i: (i, 0)),
        out_shape=jax.ShapeDtypeStruct((m, n), F32),
        compiler_params=_cparams(1),
        name="norm_matmul",
    )(x, g.reshape(1, d), *ws)


def _causal_conv_silu(staged, cw, cb, rows):
    xc = cb
    for kk in range(SSD_CONV):
        lag = SSD_CONV - 1 - kk
        src = staged if lag == 0 else pltpu.roll(staged, lag, axis=0)
        xc = xc + cw[kk:kk + 1, :] * src[SUBLANES:SUBLANES + rows, :]
    return _silu(xc)


MLP_SPLITS = 2


def _mlp_part_kernel(*refs, first, final_norm):
    if first:
        x_ref, g_ref, w1_ref, w2_ref, gf_ref, o_ref = refs
        acc_ref = x_ref
    else:
        x_ref, acc_ref, g_ref, w1_ref, w2_ref, gf_ref, o_ref = refs
    u = _rms(x_ref[...], g_ref[...]).astype(BF16)
    a = jnp.dot(u, w1_ref[...], preferred_element_type=F32)
    a = jnp.maximum(a, 0.0)
    a = (a * a).astype(BF16)
    o = acc_ref[...] + jnp.dot(a, w2_ref[...], preferred_element_type=F32)
    if final_norm:
        o = _rms(o, gf_ref[...])
    o_ref[...] = o


def _mlp(x, g, w1, w2, layer, gf, final_norm, tm=256):
    m, d = x.shape
    f = w1.shape[2]
    fs = f // MLP_SPLITS
    tm = _row_tile(m, tm)
    rows = pl.BlockSpec((tm, d), lambda i: (i, 0))
    acc = x
    for h in range(MLP_SPLITS):
        first = h == 0
        last = h == MLP_SPLITS - 1
        w_specs = [pl.BlockSpec((None, d, fs), lambda i, h=h: (layer, 0, h),
                                pipeline_mode=pl.Buffered(1)),
                   pl.BlockSpec((None, fs, d), lambda i, h=h: (layer, h, 0),
                                pipeline_mode=pl.Buffered(1))]
        acc = pl.pallas_call(
            functools.partial(_mlp_part_kernel, first=first, final_norm=final_norm and last),
            grid=(m // tm,),
            in_specs=([rows] if first else [rows, rows]) + [_resident((1, d))] + w_specs
                     + [_resident((1, d))],
            out_specs=rows,
            out_shape=jax.ShapeDtypeStruct((m, d), F32),
            compiler_params=_cparams(1),
            name="mlp",
        )(*([x] if first else [x, acc]), g.reshape(1, d), w1, w2, gf.reshape(1, d))
    return acc


def _matmul_residual_kernel(x_ref, a_ref, w_ref, o_ref):
    o_ref[...] = x_ref[...] + jnp.dot(a_ref[...], w_ref[...], preferred_element_type=F32)


def _matmul_residual(x, a, w, layer, tm=512):
    m, n = x.shape
    kdim = a.shape[1]
    tm = _row_tile(m, tm)
    return pl.pallas_call(
        _matmul_residual_kernel,
        grid=(m // tm,),
        in_specs=[pl.BlockSpec((tm, n), lambda i: (i, 0)),
                  pl.BlockSpec((tm, kdim), lambda i: (i, 0)),
                  _layer_resident(w, layer)],
        out_specs=pl.BlockSpec((tm, n), lambda i: (i, 0)),
        out_shape=jax.ShapeDtypeStruct((m, n), F32),
        compiler_params=_cparams(1),
        name="matmul_residual",
    )(x, a, w)


def _glu_residual_kernel(x_ref, a_ref, w_ref, o_ref):
    a = a_ref[...]
    n = o_ref.shape[1]
    ga = jnp.dot(a, w_ref[:, :n], preferred_element_type=F32)
    gb = jnp.dot(a, w_ref[:, n:], preferred_element_type=F32)
    o_ref[...] = x_ref[...] + ga * (0.5 + 0.5 * jnp.tanh(0.5 * gb))


def _glu_residual(x, a, w, layer, tm=512):
    m, n = x.shape
    kdim = a.shape[1]
    tm = _row_tile(m, tm)
    return pl.pallas_call(
        _glu_residual_kernel,
        grid=(m // tm,),
        in_specs=[pl.BlockSpec((tm, n), lambda i: (i, 0)),
                  pl.BlockSpec((tm, kdim), lambda i: (i, 0)),
                  _layer_resident(w, layer)],
        out_specs=pl.BlockSpec((tm, n), lambda i: (i, 0)),
        out_shape=jax.ShapeDtypeStruct((m, n), F32),
        compiler_params=_cparams(1),
        name="glu_residual",
    )(x, a, w)


S5_SCAN_CHUNKS = 2


S5_STEP_SLABS = 4


def _s5_kernel(x_ref, gn_ref, wb_ref, wc_ref, lam_ref, d_ref, h0_ref, g_ref, hf_ref,
               u_ref, hbuf_ref, hs_ref, *, bsz, tt, nk):
    k = pl.program_id(0)
    sp = pl.program_id(1)
    ns = S5_STEP_SLABS
    wch = ns * S5_SLAB_CH

    @pl.when((k == 0) & (sp == 0))
    def _():
        hs_ref[...] = h0_ref[...]

    @pl.when(sp == 0)
    def _():
        un = _rms(x_ref[...], gn_ref[...])
        for j in range(D_MODEL // wch):
            u_ref[j] = un[:, :, j * wch:(j + 1) * wch]

    u = jnp.swapaxes(u_ref[sp], 0, 1).reshape(tt * bsz, wch)
    for i in range(ns):
        ui = u[:, i * S5_SLAB_CH:(i + 1) * S5_SLAB_CH].astype(BF16)
        hbuf_ref[i] = jnp.dot(ui, wb_ref[sp * ns + i], preferred_element_type=F32)

    cw = S5_SLAB_ST // S5_SCAN_CHUNKS
    for i in range(ns):
        slab = sp * ns + i
        for q in range(S5_SCAN_CHUNKS):
            re = slice(q * cw, (q + 1) * cw)
            im = slice(S5_SLAB_ST + q * cw, S5_SLAB_ST + (q + 1) * cw)
            lr = lam_ref[slab, 0:1, re]
            li = lam_ref[slab, 1:2, re]

            def step(t, carry, i=i, re=re, im=im, lr=lr, li=li):
                hr, hi = carry
                rows = pl.ds(pl.multiple_of(t * bsz, bsz), bsz)
                nr = lr * hr - li * hi + hbuf_ref[i, rows, re]
                ni = lr * hi + li * hr + hbuf_ref[i, rows, im]
                hbuf_ref[i, rows, re] = nr
                hbuf_ref[i, rows, im] = ni
                return nr, ni

            hr, hi = lax.fori_loop(0, tt, step, (hs_ref[slab, :, re], hs_ref[slab, :, im]), unroll=True)
            hs_ref[slab, :, re] = hr
            hs_ref[slab, :, im] = hi

    d = d_ref[sp]
    for i in range(ns):
        cs = slice(i * S5_SLAB_CH, (i + 1) * S5_SLAB_CH)
        y = jnp.dot(hbuf_ref[i].astype(BF16), wc_ref[sp * ns + i], preferred_element_type=F32)
        y = jax.nn.gelu(y + d[:, cs] * u[:, cs])
        y = jnp.swapaxes(y.reshape(tt, bsz, S5_SLAB_CH), 0, 1)
        g_ref[:, :, cs] = y.astype(BF16)

    @pl.when((k == nk - 1) & (sp == S5_SLABS // ns - 1))
    def _():
        hf_ref[...] = hs_ref[...]


def _s5_core(x3, norm_g, wb, wc, lam, d, h0):
    bsz, tlen, _ = x3.shape
    tt = min(tlen, 512 // bsz)
    rows = tt * bsz
    nk = tlen // tt
    ns = S5_STEP_SLABS
    nsp = S5_SLABS // ns
    full = lambda shape: pl.BlockSpec(shape, lambda k, s: (0,) * len(shape),
                                      pipeline_mode=pl.Buffered(1))
    return pl.pallas_call(
        functools.partial(_s5_kernel, bsz=bsz, tt=tt, nk=nk),
        grid=(nk, nsp),
        in_specs=[pl.BlockSpec((bsz, tt, D_MODEL), lambda k, s: (0, k, 0)),
                  full((1, D_MODEL)),
                  full((S5_SLABS, S5_SLAB_CH, 2 * S5_SLAB_ST)),
                  full((S5_SLABS, 2 * S5_SLAB_ST, S5_SLAB_CH)),
                  full((S5_SLABS, 2, S5_SLAB_ST)),
                  full((nsp, 1, ns * S5_SLAB_CH)),
                  full((S5_SLABS, bsz, 2 * S5_SLAB_ST))],
        out_specs=[pl.BlockSpec((bsz, tt, ns * S5_SLAB_CH), lambda k, s: (0, k, s)),
                   pl.BlockSpec((S5_SLABS, bsz, 2 * S5_SLAB_ST), lambda k, s: (0, 0, 0))],
        out_shape=[jax.ShapeDtypeStruct((bsz, tlen, D_MODEL), BF16),
                   jax.ShapeDtypeStruct((S5_SLABS, bsz, 2 * S5_SLAB_ST), F32)],
        scratch_shapes=[pltpu.VMEM((nsp, bsz, tt, ns * S5_SLAB_CH), F32),
                        pltpu.VMEM((ns, rows, 2 * S5_SLAB_ST), F32),
                        pltpu.VMEM((S5_SLABS, bsz, 2 * S5_SLAB_ST), F32)],
        compiler_params=_cparams(2),
        name="s5_core",
    )(x3, norm_g.reshape(1, D_MODEL), wb, wc, lam, d.reshape(nsp, 1, ns * S5_SLAB_CH), h0)


def _s5_params(lam_re, lam_im, log_dt, b_re, b_im, c_re, c_im):
    dt = jnp.exp(log_dt)[:, None]
    er = jnp.exp(lam_re * dt)
    lbr = er * jnp.cos(lam_im * dt)
    lbi = er * jnp.sin(lam_im * dt)
    den = lam_re * lam_re + lam_im * lam_im
    qr = ((lbr - 1.0) * lam_re + lbi * lam_im) / den
    qi = (lbi * lam_re - (lbr - 1.0) * lam_im) / den
    bbr = qr[..., None] * b_re - qi[..., None] * b_im
    bbi = qr[..., None] * b_im + qi[..., None] * b_re
    gl = S5_GROUPS // S5_SLABS
    eye = jnp.eye(gl, dtype=F32)

    def pack_b(bb):
        bb = bb.reshape(S5_SLABS, gl, S5_STATE, S5_GROUP)
        w = jnp.einsum("sgpc,gh->sgchp", bb, eye)
        return w.reshape(S5_SLABS, S5_SLAB_CH, S5_SLAB_ST)

    def pack_c(cc):
        cc = cc.reshape(S5_SLABS, gl, S5_GROUP, S5_STATE)
        w = jnp.einsum("sgcp,gh->sgphc", cc, eye)
        return w.reshape(S5_SLABS, S5_SLAB_ST, S5_SLAB_CH)

    wb = jnp.concatenate([pack_b(bbr), pack_b(bbi)], axis=2).astype(BF16)
    wc = jnp.concatenate([pack_c(c_re), pack_c(-c_im)], axis=1).astype(BF16)
    lam = jnp.stack([lbr.reshape(S5_SLABS, S5_SLAB_ST), lbi.reshape(S5_SLABS, S5_SLAB_ST)], axis=1)
    return wb, wc, lam


def _s5_state_in(h):
    bsz = h.shape[0]
    return h.reshape(bsz, S5_SLABS, S5_SLAB_ST).transpose(1, 0, 2)


def _s5_state_out(h):
    bsz = h.shape[1]
    return h.transpose(1, 0, 2).reshape(bsz, S5_GROUPS, S5_STATE)


def _s5_layer(x, bsz, tlen, h_re, h_im, norm_g, prm, d_skip, w_glu, layer):
    wb, wc, lam = prm
    h0 = jnp.concatenate([_s5_state_in(h_re), _s5_state_in(h_im)], axis=2)
    g, hf = _s5_core(x.reshape(bsz, tlen, D_MODEL), norm_g, wb, wc, lam, d_skip, h0)
    x = _glu_residual(x, g.reshape(bsz * tlen, D_MODEL), w_glu, layer)
    return x, _s5_state_out(hf[:, :, :S5_SLAB_ST]), _s5_state_out(hf[:, :, S5_SLAB_ST:])


def _softplus(x):
    return jnp.maximum(x, 0.0) + jnp.log(1.0 + jnp.exp(-jnp.abs(x)))


def _lane_bcast(v, col, width):
    return jnp.broadcast_to(v[:, col:col + 1], (v.shape[0], width))


def _split3(x):
    hi = x.astype(BF16)
    r = x - hi.astype(F32)
    mid = r.astype(BF16)
    lo = (r - mid.astype(F32)).astype(BF16)
    return jnp.concatenate([hi, mid, lo], axis=1)


def _ssd_kernel(xbc_ref, z0_ref, z1_ref, dt_ref, sst_ref, dtb_ref, alog_ref, dsk_ref, nw_ref,
                cst_ref, cw_ref, cb_ref, hx_ref, y_ref, sso_ref, cso_ref,
                st_ref, yb_ref, xp_ref, xs_ref, *, clen, nk):
    k = pl.program_id(1)

    @pl.when(k == 0)
    def _():
        st_ref[...] = sst_ref[...].reshape(SSD_INNER, SSD_STATE)
        xp_ref[0:SUBLANES, :] = cst_ref[0]

    xp_ref[SUBLANES:SUBLANES + clen, :] = xbc_ref[...]
    xs_ref[...] = _causal_conv_silu(xp_ref[...], cw_ref[...], cb_ref[...], clen)
    xp_ref[0:SUBLANES, :] = xp_ref[clen:clen + SUBLANES, :]

    bmat = xs_ref[:, SSD_INNER:SSD_INNER + SSD_GROUPS * SSD_STATE].astype(BF16)
    cmat = xs_ref[:, SSD_INNER + SSD_GROUPS * SSD_STATE:].astype(BF16)

    dt = _softplus(dt_ref[...] + dtb_ref[...])
    da = dt * (-jnp.exp(alog_ref[...]))
    ri = lax.broadcasted_iota(jnp.int32, (clen, clen), 0)
    ci = lax.broadcasted_iota(jnp.int32, (clen, clen), 1)
    causal = ri >= ci
    acum = jnp.dot(causal.astype(F32), da, precision=HIGHEST, preferred_element_type=F32)
    eye = (lax.broadcasted_iota(jnp.int32, (LANES, LANES), 0)
           == lax.broadcasted_iota(jnp.int32, (LANES, LANES), 1)).astype(F32)
    acum_t = lax.dot_general(eye, acum, NT_DIMS, precision=HIGHEST, preferred_element_type=F32)
    dt_t = lax.dot_general(eye, dt, NT_DIMS, precision=HIGHEST, preferred_element_type=F32)
    eacum = jnp.exp(acum)
    tailw = jnp.exp(acum[clen - 1:clen, :] - acum) * dt
    sdec = jnp.exp(acum_t[:, clen - 1:clen])
    acum2 = acum * LOG2_E
    acum2_t = acum_t * LOG2_E

    lo_lane = lax.broadcasted_iota(jnp.int32, (clen, LANES), 1) < SSD_HEAD_DIM
    ea_all = jnp.dot(_split3(eacum), hx_ref[...], preferred_element_type=F32)
    tw_all = jnp.dot(_split3(tailw), hx_ref[...], preferred_element_type=F32)
    lo_row = lax.broadcasted_iota(jnp.int32, (LANES, SSD_STATE), 0) < SSD_HEAD_DIM

    for g in range(SSD_GROUPS):
        gs = slice(g * SSD_STATE, (g + 1) * SSD_STATE)
        cg = cmat[:, gs]
        bg = bmat[:, gs]
        cbm = lax.dot_general(cg, bg, NT_DIMS, preferred_element_type=F32)
        cbm = jnp.where(causal, cbm, 0.0)
        sg = st_ref[g * SSD_GROUP_CH:(g + 1) * SSD_GROUP_CH, :].astype(BF16)
        chs = lax.dot_general(cg, sg, NT_DIMS, preferred_element_type=F32)
        for j in range(SSD_GROUP_CH // LANES):
            h0 = (g * SSD_GROUP_CH + j * LANES) // SSD_HEAD_DIM
            h1 = h0 + 1
            cs = slice(g * SSD_GROUP_CH + j * LANES, g * SSD_GROUP_CH + (j + 1) * LANES)
            xf = xs_ref[:, cs]
            xb = xf.astype(BF16)
            ys = []
            for h in (h0, h1):
                diff = jnp.minimum(_lane_bcast(acum2, h, clen) - acum2_t[h:h + 1, :], 0.0)
                w = cbm * jnp.exp2(diff) * dt_t[h:h + 1, :]
                ys.append(jnp.dot(w.astype(BF16), xb, preferred_element_type=F32))
            y = jnp.where(lo_lane, ys[0], ys[1])
            y = y + chs[:, j * LANES:(j + 1) * LANES] * ea_all[:, cs] + dsk_ref[:, cs] * xf
            yb_ref[:, cs] = y
            txb = (xf * tw_all[:, cs]).astype(BF16)
            ds = lax.dot_general(txb, bg, (((0,), (0,)), ((), ())), preferred_element_type=F32)
            dec = jnp.where(lo_row,
                            jnp.broadcast_to(sdec[h0:h0 + 1, :], (LANES, SSD_STATE)),
                            jnp.broadcast_to(sdec[h1:h1 + 1, :], (LANES, SSD_STATE)))
            st_ref[cs, :] = st_ref[cs, :] * dec + ds

    half = SSD_INNER // 2
    for hf, z_ref in enumerate((z0_ref, z1_ref)):
        z = z_ref[...]
        yh = yb_ref[:, hf * half:(hf + 1) * half] * _silu(z)
        for gg in range(half // SSD_GROUP_CH):
            yg = yh[:, gg * SSD_GROUP_CH:(gg + 1) * SSD_GROUP_CH]
            ms = jnp.mean(yg * yg, axis=-1, keepdims=True)
            cs = slice(hf * half + gg * SSD_GROUP_CH, hf * half + (gg + 1) * SSD_GROUP_CH)
            y_ref[:, cs] = (yg * lax.rsqrt(ms + RMS_EPS) * nw_ref[:, cs]).astype(BF16)

    @pl.when(k == nk - 1)
    def _():
        sso_ref[...] = st_ref[...].reshape(SSD_HEADS, SSD_HEAD_DIM, SSD_STATE)
        cso_ref[0] = xp_ref[0:SUBLANES, :]


def _ssd_kernel_with_acc(*refs, **kw):
    _ssd_kernel(*refs[:13], *refs[14:], **kw)


def _ssd_core(xbc, zdt, sst_all, sso_all, layer, cst, conv_w, conv_b, dt_bias, a_log, d_exp, norm_w,
              bsz, tlen):
    clen = min(tlen, 128)
    nk = tlen // clen
    zblk = SSD_INNER // 2
    dt_off = SSD_INNER // LANES
    row = lambda b, k: b * nk + k
    pad = lambda v: jnp.pad(v, (0, LANES - SSD_HEADS)).reshape(1, LANES)
    const = lambda shape: pl.BlockSpec(shape, lambda b, k: (0,) * len(shape))
    per_seq = lambda shape: pl.BlockSpec((1,) + shape, lambda b, k: (b,) + (0,) * len(shape))
    state = (SSD_HEADS, SSD_HEAD_DIM, SSD_STATE)
    layer_state = pl.BlockSpec((None, None) + state, lambda b, k: (layer, b, 0, 0, 0))
    in_specs = [pl.BlockSpec((clen, SSD_CONV_DIM), lambda b, k: (row(b, k), 0)),
                pl.BlockSpec((clen, zblk), lambda b, k: (row(b, k), 0)),
                pl.BlockSpec((clen, zblk), lambda b, k: (row(b, k), 1)),
                pl.BlockSpec((clen, LANES), lambda b, k: (row(b, k), dt_off)),
                layer_state,
                const((1, LANES)), const((1, LANES)), const((1, SSD_INNER)), const((1, SSD_INNER)),
                per_seq((SUBLANES, SSD_CONV_DIM)), const((SSD_CONV, SSD_CONV_DIM)),
                const((1, SSD_CONV_DIM)), const((3 * LANES, SSD_INNER))]
    head_of = jnp.arange(SSD_INNER, dtype=jnp.int32) // SSD_HEAD_DIM
    hx = (jnp.arange(LANES, dtype=jnp.int32)[:, None] == head_of[None, :]).astype(BF16)
    args = [xbc, zdt, zdt, zdt, sst_all, pad(dt_bias), pad(a_log), d_exp.reshape(1, SSD_INNER),
            norm_w.reshape(1, SSD_INNER), cst, conv_w, conv_b.reshape(1, SSD_CONV_DIM),
            jnp.concatenate([hx, hx, hx], axis=0)]
    body, aliases = _ssd_kernel, {}
    if sso_all is not None:
        body, aliases = _ssd_kernel_with_acc, {len(args): 1}
        in_specs.append(pl.BlockSpec(memory_space=pl.ANY))
        args.append(sso_all)
    return pl.pallas_call(
        functools.partial(body, clen=clen, nk=nk),
        grid=(bsz, nk),
        in_specs=in_specs,
        out_specs=[pl.BlockSpec((clen, SSD_INNER), lambda b, k: (row(b, k), 0)),
                   layer_state,
                   per_seq((SUBLANES, SSD_CONV_DIM))],
        out_shape=[jax.ShapeDtypeStruct((bsz * tlen, SSD_INNER), BF16),
                   jax.ShapeDtypeStruct(sst_all.shape, F32),
                   jax.ShapeDtypeStruct((bsz, SUBLANES, SSD_CONV_DIM), F32)],
        scratch_shapes=[pltpu.VMEM((SSD_INNER, SSD_STATE), F32),
                        pltpu.VMEM((clen, SSD_INNER), F32),
                        pltpu.VMEM((clen + SUBLANES, SSD_CONV_DIM), F32),
                        pltpu.VMEM((clen, SSD_CONV_DIM), F32)],
        input_output_aliases=aliases,
        compiler_params=_cparams(2),
        name="ssd_core",
    )(*args)


W_IN_TILE = 256


def _ssd_in_weight_kernel(wt_ref, xbc_ref, z_ref, dt_ref, *, nz, nxbc):
    j = pl.program_id(1)
    w = wt_ref[...].T

    @pl.when(j < nz)
    def _():
        z_ref[...] = w.astype(BF16)

    @pl.when((j >= nz) & (j < nz + nxbc))
    def _():
        xbc_ref[...] = w.astype(BF16)

    @pl.when(j == nz + nxbc)
    def _():
        col = lax.broadcasted_iota(jnp.int32, (w.shape[0], LANES), 1)
        dt_ref[...] = jnp.where(col < SSD_HEADS, w[:, :LANES], 0.0).astype(BF16)


def _ssd_in_weight(w_in):
    nl, d, n = w_in.shape
    nz = SSD_INNER // W_IN_TILE
    nxbc = SSD_CONV_DIM // W_IN_TILE
    wt = jnp.swapaxes(w_in, 1, 2)
    out = lambda cols: (None, d, cols)
    return pl.pallas_call(
        functools.partial(_ssd_in_weight_kernel, nz=nz, nxbc=nxbc),
        grid=(nl, pl.cdiv(n, W_IN_TILE)),
        in_specs=[pl.BlockSpec((None, W_IN_TILE, d), lambda l, j: (l, j, 0))],
        out_specs=[pl.BlockSpec(out(W_IN_TILE), lambda l, j: (l, 0, jnp.clip(j - nz, 0, nxbc - 1))),
                   pl.BlockSpec(out(W_IN_TILE), lambda l, j: (l, 0, jnp.minimum(j, nz - 1))),
                   pl.BlockSpec(out(LANES), lambda l, j: (l, 0, 0))],
        out_shape=[jax.ShapeDtypeStruct((nl, d, SSD_CONV_DIM), BF16),
                   jax.ShapeDtypeStruct((nl, d, SSD_INNER), BF16),
                   jax.ShapeDtypeStruct((nl, d, LANES), BF16)],
        compiler_params=_cparams(2),
        name="ssd_in_weight",
    )(wt)


def _ssd_layer(x, bsz, tlen, conv_st, ssm_all, sso_all, layer, norm_g, w_in_p, conv_w, conv_b,
               dt_bias, a_log, d_skip, norm_w, w_out):
    hist = SSD_CONV - 1
    cst = jnp.pad(conv_st, ((0, 0), (SUBLANES - hist, 0), (0, 0)))
    d_exp = jnp.repeat(d_skip, SSD_HEAD_DIM)
    w_xbc, w_z, w_dt = w_in_p
    xbc = _norm_matmul(x, norm_g, [w_xbc], layer)
    zdt = _norm_matmul(x, norm_g, [w_z, w_dt], layer)
    y, sso_all, cso = _ssd_core(xbc, zdt, ssm_all, sso_all, layer, cst, conv_w, conv_b, dt_bias,
                                a_log, d_exp, norm_w, bsz, tlen)
    x = _matmul_residual(x, y, w_out, layer)
    return x, cso[:, SUBLANES - hist:], sso_all


def _run(x3, s5_re, s5_im, conv_st, ssm_st, p):
    bsz, tlen, d = x3.shape
    x = x3.reshape(bsz * tlen, d)
    s5_re_out, s5_im_out, conv_out = [], [], []
    ssm_out = None
    for i in range(DEPTH):
        j = i // 2
        if i % 2 == 0:
            x, hr, hi = _s5_layer(x, bsz, tlen, s5_re[j], s5_im[j], p["norm_mix"][i], p["s5"][j],
                                  p["s5_d"][j], p["s5_w_glu"], j)
            s5_re_out.append(hr)
            s5_im_out.append(hi)
        else:
            x, c, ssm_out = _ssd_layer(
                x, bsz, tlen, conv_st[j], ssm_st, ssm_out, j, p["norm_mix"][i], p["ssd_w_in"],
                p["ssd_conv_w"][j], p["ssd_conv_b"][j], p["ssd_dt_bias"][j], p["ssd_a_log"][j],
                p["ssd_d"][j], p["ssd_norm"][j], p["ssd_w_out"])
            conv_out.append(c)
        x = _mlp(x, p["norm_mlp"][i], p["mlp_w1"], p["mlp_w2"], i, p["norm_final"],
                 final_norm=(i == DEPTH - 1))
    return (x.reshape(bsz, tlen, d), jnp.stack(s5_re_out), jnp.stack(s5_im_out),
            jnp.stack(conv_out), ssm_out)


def kernel(x_prompt, x_sample, state_s5_re, state_s5_im, state_ssd_conv, state_ssd_ssm, norm_mix, norm_mlp, norm_final, s5_lam_re, s5_lam_im, s5_log_dt, s5_b_re, s5_b_im, s5_c_re, s5_c_im, s5_d, s5_w_glu, ssd_w_in, ssd_conv_w, ssd_conv_b, ssd_dt_bias, ssd_a_log, ssd_d, ssd_norm, ssd_w_out, mlp_w1, mlp_w2):
    n_s5 = s5_lam_re.shape[0]
    n_ssd = ssd_w_in.shape[0]
    p = dict(
        norm_mix=norm_mix, norm_mlp=norm_mlp, norm_final=norm_final,
        s5=[_s5_params(s5_lam_re[j], s5_lam_im[j], s5_log_dt[j], s5_b_re[j], s5_b_im[j],
                       s5_c_re[j], s5_c_im[j]) for j in range(n_s5)],
        s5_d=s5_d, s5_w_glu=s5_w_glu.astype(BF16),
        ssd_w_in=_ssd_in_weight(ssd_w_in),
        ssd_conv_w=ssd_conv_w, ssd_conv_b=ssd_conv_b, ssd_dt_bias=ssd_dt_bias, ssd_a_log=ssd_a_log,
        ssd_d=ssd_d, ssd_norm=ssd_norm, ssd_w_out=ssd_w_out.astype(BF16),
        mlp_w1=mlp_w1.astype(BF16), mlp_w2=mlp_w2.astype(BF16),
    )
    bp = x_prompt.shape[0]
    zeros = lambda shape: jnp.zeros(shape, F32)
    y_p, s5_re_p, s5_im_p, conv_p, ssm_p = _run(
        x_prompt,
        zeros((n_s5, bp, S5_GROUPS, S5_STATE)), zeros((n_s5, bp, S5_GROUPS, S5_STATE)),
        zeros((n_ssd, bp, SSD_CONV - 1, SSD_CONV_DIM)),
        zeros((n_ssd, bp, SSD_HEADS, SSD_HEAD_DIM, SSD_STATE)), p)
    y_s, s5_re_s, s5_im_s, conv_s, ssm_s = _run(
        x_sample, state_s5_re, state_s5_im, state_ssd_conv, state_ssd_ssm, p)
    return (y_p, y_s, s5_re_p, s5_im_p, conv_p, ssm_p, s5_re_s, s5_im_s, conv_s, ssm_s)
```
